```python
import math
import jax, jax.numpy as jnp
from jax import lax
import numpy as np

D_MODEL = 1024
BATCH = 32
SEQ = 256
DEPTH = 4
DEC_BATCH = 8
DEC_SEQ = 4096
PAST_LEN = 256

GRID_W = 64
CHUNK = 128
LN_EPS = 1e-5
W_HY = D_MODEL
HY_ORDER = 2
POS_EMB = 33
FILT_HID = 64
HY_SHORT = 3
HY_TARGET = 1e-2
HY_FAST = 0.3
HY_SLOW = 1.5
D_INNER = 2 * D_MODEL
P_SSD = 64
H_SSD = D_INNER // P_SSD
G_SSD = 4
N_SSD = 128
SSD_CONV = 3
CONV_CH = D_INNER + 2 * G_SSD * N_SSD
H_RET = 4
DK_RET = 256
DV_RET = 256
W_QK = H_RET * DK_RET
W_RET = H_RET * DV_RET
ROPE_BASE = 10000.0
N_BRANCH = 3
N_EXPERTS = 16
EC_CAPACITY = 2
D_EXPERT = D_MODEL
ALPHA = (2 * DEPTH) ** 0.25
BETA = (8 * DEPTH) ** -0.25
P_IN = 3 * W_HY + D_INNER + CONV_CH + 2 * H_SSD + 2 * W_QK + 2 * W_RET + N_BRANCH * D_MODEL

kernel_name = 'hyena_ssd_retention_ec_moe_diffusion_step'


def layer_norm(x, g=None, b=None):
    xf = x.astype(jnp.float32)
    mu = jnp.mean(xf, axis=-1, keepdims=True)
    var = jnp.mean(jnp.square(xf - mu), axis=-1, keepdims=True)
    y = (xf - mu) * lax.rsqrt(var + LN_EPS)
    if g is not None:
        y = y * g + b
    return y.astype(x.dtype)


def rms_normalize(x):
    xf = x.astype(jnp.float32)
    return xf * lax.rsqrt(jnp.mean(xf * xf, axis=-1, keepdims=True) + LN_EPS)


def flip_seq(t):
    return jnp.flip(t, axis=1)


def centred_dwconv(u, w, bias):
    K, C = w.shape
    y = lax.conv_general_dilated(u, w[:, None, :].astype(u.dtype), window_strides=(1,),
                                 padding=[(K // 2, K // 2)],
                                 dimension_numbers=('NWC', 'WIO', 'NWC'),
                                 feature_group_count=C)
    return y + bias


def axial_rope(x, row_pos, col_pos):
    half = x.shape[-1] // 2
    quarter = half // 2
    inv_freq = ROPE_BASE ** (-jnp.arange(quarter, dtype=jnp.float32) / quarter)

    def rot(xp, pos):
        ang = pos[:, None].astype(jnp.float32) * inv_freq[None]
        cos = jnp.cos(ang)[None, :, None]
        sin = jnp.sin(ang)[None, :, None]
        x1, x2 = xp[..., :quarter], xp[..., quarter:]
        return jnp.concatenate([x1 * cos - x2 * sin, x2 * cos + x1 * sin], axis=-1)

    out = jnp.concatenate([rot(x[..., :half], row_pos), rot(x[..., half:], col_pos)], axis=-1)
    return out.astype(x.dtype)


def chunked_scan(x, log_a, Bm, Cm, init_state):
    f32 = jnp.float32
    b, L, h, p = x.shape
    g, n = Bm.shape[-2:]
    r = h // g
    nc = L // CHUNK
    xc = x.astype(f32).reshape(b, nc, CHUNK, g, r, p)
    Bc = Bm.astype(f32).reshape(b, nc, CHUNK, g, n)
    Cc = Cm.astype(f32).reshape(b, nc, CHUNK, g, n)
    a = log_a.astype(f32).reshape(b, nc, CHUNK, g, r)
    a_cum = jnp.moveaxis(jnp.cumsum(a, axis=2), 2, -1)
    causal = jnp.tril(jnp.ones((CHUNK, CHUNK), dtype=bool))
    decay = jnp.exp(jnp.where(causal, a_cum[..., :, None] - a_cum[..., None, :], -jnp.inf))
    scores = jnp.einsum('bcign,bcjgn->bcgij', Cc, Bc)
    w_intra = scores[:, :, :, None] * decay
    y_diag = jnp.einsum('bcgrij,bcjgrp->bcigrp', w_intra, xc)
    end_decay = jnp.exp(a_cum[..., -1:] - a_cum)
    chunk_states = jnp.einsum('bcjgn,bcgrj,bcjgrp->bcgrpn', Bc, end_decay, xc)
    chunk_decay = jnp.exp(a_cum[..., -1])

    def step(S, inp):
        st, dec = inp
        return S * dec[..., None, None] + st, S

    S0 = init_state.astype(f32).reshape(b, g, r, p, n)
    S_fin, S_in = lax.scan(step, S0, (jnp.moveaxis(chunk_states, 1, 0), jnp.moveaxis(chunk_decay, 1, 0)))
    S_in = jnp.moveaxis(S_in, 0, 1)
    y_off = jnp.einsum('bcign,bcgrpn,bcgri->bcigrp', Cc, S_in, jnp.exp(a_cum))
    y = (y_diag + y_off).reshape(b, L, h, p).astype(x.dtype)
    return y, S_fin.reshape(b, h, p, n).astype(x.dtype)


def hyena_filters(L, w1, b1, w2, b2, w3, sin_freq):
    f32 = jnp.float32
    t = jnp.arange(L, dtype=f32)
    t_norm = t / max(L - 1, 1)
    bands = (POS_EMB - 1) // 2
    freqs = jnp.linspace(1e-4, bands - 1, bands, dtype=f32)
    ang = 2.0 * math.pi * t[:, None] * freqs[None] / L
    z = jnp.concatenate([t_norm[:, None], jnp.cos(ang), -jnp.sin(ang)], axis=-1)
    hdn = jnp.sin(sin_freq[0] * (z @ w1 + b1))
    hdn = jnp.sin(sin_freq[1] * (hdn @ w2 + b2))
    k = (hdn @ w3).reshape(L, HY_ORDER, 2, W_HY)
    deltas = jnp.linspace(math.log(HY_TARGET) / HY_SLOW, math.log(HY_TARGET) / HY_FAST, W_HY, dtype=f32)
    window = jnp.exp(-t_norm[:, None] * jnp.abs(deltas)[None])
    return k * window[:, None, None, :]


def bidir_fftconv(u, k_fwd, k_bwd, bias):
    L = u.shape[1]
    k = jnp.concatenate([k_fwd, jnp.zeros_like(k_fwd[:1]), jnp.flip(k_bwd[1:], axis=0)], axis=0)
    u_f = jnp.fft.rfft(u.astype(jnp.float32), n=2 * L, axis=1)
    k_f = jnp.fft.rfft(k.astype(jnp.float32), axis=0)
    y = jnp.fft.irfft(u_f * k_f[None], n=2 * L, axis=1)[:, :L]
    return (y + u * bias).astype(u.dtype)


def hyena_branch(u, conv_w, conv_b, filt, bias):
    uc = centred_dwconv(u, conv_w, conv_b)
    v, x1, x2 = jnp.split(uc, 3, axis=-1)
    z = x1 * bidir_fftconv(v, filt[:, 0, 0], filt[:, 0, 1], bias[0])
    return x2 * bidir_fftconv(z, filt[:, 1, 0], filt[:, 1, 1], bias[1])


def ssd_branch(u_z, u_xbc, u_dt, conv_w, conv_b, dt_bias, a_log, d_skip, norm_w, init):
    b, L, _ = u_z.shape
    xbc = jax.nn.silu(centred_dwconv(u_xbc, conv_w, conv_b))
    xs, Bm, Cm = jnp.split(xbc, [D_INNER, D_INNER + G_SSD * N_SSD], axis=-1)
    xs = xs.reshape(b, L, H_SSD, P_SSD)
    Bm = Bm.reshape(b, L, G_SSD, N_SSD)
    Cm = Cm.reshape(b, L, G_SSD, N_SSD)
    dt = jax.nn.softplus(u_dt.reshape(b, L, 2, H_SSD).astype(jnp.float32) + dt_bias)
    log_a = dt * (-jnp.exp(a_log.astype(jnp.float32)))
    y_f, s_f = chunked_scan(xs * dt[:, :, 0, :, None], log_a[:, :, 0], Bm, Cm, init[:, 0])
    y_b, s_b = chunked_scan(flip_seq(xs * dt[:, :, 1, :, None]), flip_seq(log_a[:, :, 1]),
                            flip_seq(Bm), flip_seq(Cm), init[:, 1])
    y = y_f + flip_seq(y_b) + xs * d_skip[:, None]
    y = (y.reshape(b, L, D_INNER) * jax.nn.silu(u_z)).reshape(b, L, G_SSD, D_INNER // G_SSD)
    y = rms_normalize(y).reshape(b, L, D_INNER) * norm_w
    return y.astype(u_z.dtype), jnp.stack([s_f, s_b], axis=1)


def retention_branch(u_q, u_k, u_v, u_g, decay_raw, gn_w, init, grid_pos):
    b, L, _ = u_q.shape
    q = u_q.reshape(b, L, H_RET, DK_RET)
    k = u_k.reshape(b, L, H_RET, DK_RET)
    v = u_v.reshape(b, L, H_RET, DV_RET)
    if grid_pos is not None:
        q = axial_rope(q, grid_pos[0], grid_pos[1])
        k = axial_rope(k, grid_pos[0], grid_pos[1])
    k = k * DK_RET ** -0.5
    log_g = -jnp.exp(decay_raw.astype(jnp.float32))
    la_f = jnp.broadcast_to(log_g[0], (b, L, H_RET))
    la_b = jnp.broadcast_to(log_g[1], (b, L, H_RET))
    o_f, s_f = chunked_scan(v, la_f, k, q, init[:, 0])
    o_b, s_b = chunked_scan(flip_seq(v), la_b, flip_seq(k), flip_seq(q), init[:, 1])
    o = layer_norm(o_f + flip_seq(o_b)).reshape(b, L, W_RET) * gn_w
    return (o * jax.nn.silu(u_g)).astype(u_q.dtype), jnp.stack([s_f, s_b], axis=1)


def token_mixers(h, P, l, grid_pos, ssd_init, ret_init):
    b, L, _ = h.shape
    sizes = (3 * W_HY, D_INNER, CONV_CH, 2 * H_SSD, W_QK, W_QK, W_RET, W_RET, N_BRANCH * D_MODEL)
    cuts = np.cumsum(sizes)[:-1].tolist()
    u_hy, u_z, u_xbc, u_dt, u_q, u_k, u_v, u_g, u_mg = jnp.split(h @ P['w_in'][l], cuts, axis=-1)
    filt = hyena_filters(L, P['hy_w1'][l], P['hy_b1'][l], P['hy_w2'][l], P['hy_b2'][l],
                         P['hy_w3'][l], P['hy_freq'][l])
    y_hy = hyena_branch(u_hy, P['hy_conv_w'][l], P['hy_conv_b'][l], filt, P['hy_bias'][l])
    y_ssd, s_ssd = ssd_branch(u_z, u_xbc, u_dt, P['ssd_conv_w'][l], P['ssd_conv_b'][l],
                              P['ssd_dt_bias'][l], P['ssd_a_log'][l], P['ssd_d'][l],
                              P['ssd_norm_w'][l], ssd_init)
    y_ret, s_ret = retention_branch(u_q, u_k, u_v, u_g, P['ret_decay'][l], P['ret_gn_w'][l],
                                    ret_init, grid_pos)
    gates = jax.nn.sigmoid(u_mg.astype(jnp.float32)).reshape(b, L, N_BRANCH, D_MODEL)
    merged = (gates[:, :, 0] * (y_hy @ P['hy_proj'][l])
              + gates[:, :, 1] * (y_ssd @ P['ssd_proj'][l])
              + gates[:, :, 2] * (y_ret @ P['ret_proj'][l]))
    return merged.astype(h.dtype) @ P['w_out'][l], s_ssd, s_ret


def moe_ec(h, router_w, router_b, w1, w3, w2):
    b, n, d = h.shape
    cap = EC_CAPACITY * n // N_EXPERTS
    aff = jax.nn.softmax((h @ router_w + router_b).astype(jnp.float32), axis=-1)
    g_val, idx = lax.top_k(jnp.swapaxes(aff, 1, 2), cap)
    xe = jax.vmap(lambda hb, ib: hb[ib])(h, idx)
    hid = jax.nn.silu(jnp.einsum('becd,edf->becf', xe, w1)) * jnp.einsum('becd,edf->becf', xe, w3)
    ye = (jnp.einsum('becf,efd->becd', hid, w2) * g_val[..., None]).astype(h.dtype)
    return jax.vmap(lambda ib, yb: jnp.zeros((n, d), h.dtype).at[ib.reshape(-1)].add(yb.reshape(-1, d)))(idx, ye)


def trunk_layer(x, cond, P, l, grid_pos, ssd_init, ret_init):
    mod = (jax.nn.silu(cond) @ P['ada_w'][l] + P['ada_b'][l]).reshape(cond.shape[0], 1, 6, D_MODEL)
    sh1, sc1, g1, sh2, sc2, g2 = (mod[:, :, i] for i in range(6))
    h_mix = layer_norm(x) * (1 + sc1) + sh1
    mix, s_ssd, s_ret = token_mixers(h_mix, P, l, grid_pos, ssd_init, ret_init)
    x = layer_norm(ALPHA * x + g1 * mix, P['ln1_g'][l], P['ln1_b'][l])
    h_ffn = layer_norm(x) * (1 + sc2) + sh2
    ffn = moe_ec(h_ffn, P['router_w'][l], P['router_b'][l], P['exp_w1'][l], P['exp_w3'][l], P['exp_w2'][l])
    x = layer_norm(ALPHA * x + g2 * ffn, P['ln2_g'][l], P['ln2_b'][l])
    return x, s_ssd, s_ret


def setup_inputs(seed: int = 0) -> dict:
    key = jax.random.key(seed)
    ks = iter(jax.random.split(key, 48))
    f32 = jnp.float32

    def nrm(shape, scale):
        return jax.random.normal(next(ks), shape, f32) * scale

    u_dt = jax.random.uniform(next(ks), (DEPTH, 2, H_SSD), f32)
    dt0 = jnp.exp(u_dt * (math.log(0.1) - math.log(1e-3)) + math.log(1e-3))
    ret_base = jnp.log(-jnp.log(1.0 - 2.0 ** (-5.0 - jnp.arange(H_RET, dtype=f32))))
    return {
        'x_prompt': nrm((BATCH, SEQ, D_MODEL), 1.0),
        'x_sample': nrm((DEC_BATCH, DEC_SEQ, D_MODEL), 1.0),
        'c': nrm((DEC_BATCH, D_MODEL), 1.0),
        'state_ssd': nrm((DEC_BATCH, DEPTH, 2, H_SSD, P_SSD, N_SSD), 0.1),
        'state_ret': nrm((DEC_BATCH, DEPTH, 2, H_RET, DV_RET, DK_RET), 1.0),
        'c_ctx': nrm((D_MODEL,), 1.0),
        'w_in': nrm((DEPTH, D_MODEL, P_IN), D_MODEL ** -0.5),
        'ada_w': nrm((DEPTH, D_MODEL, 6 * D_MODEL), 0.5 * D_MODEL ** -0.5),
        'ada_b': nrm((DEPTH, 6 * D_MODEL), 0.01),
        'hy_conv_w': nrm((DEPTH, HY_SHORT, 3 * W_HY), HY_SHORT ** -0.5),
        'hy_conv_b': nrm((DEPTH, 3 * W_HY), 0.01),
        'hy_w1': nrm((DEPTH, POS_EMB, FILT_HID), POS_EMB ** -0.5),
        'hy_b1': nrm((DEPTH, FILT_HID), 0.1),
        'hy_w2': nrm((DEPTH, FILT_HID, FILT_HID), FILT_HID ** -0.5),
        'hy_b2': nrm((DEPTH, FILT_HID), 0.1),
        'hy_w3': nrm((DEPTH, FILT_HID, HY_ORDER * 2 * W_HY), 0.1 * FILT_HID ** -0.5),
        'hy_freq': 1.0 + nrm((DEPTH, 2, FILT_HID), 0.01),
        'hy_bias': nrm((DEPTH, HY_ORDER, W_HY), 1.0),
        'hy_proj': nrm((DEPTH, W_HY, D_MODEL), W_HY ** -0.5),
        'ssd_conv_w': nrm((DEPTH, SSD_CONV, CONV_CH), SSD_CONV ** -0.5),
        'ssd_conv_b': nrm((DEPTH, CONV_CH), 0.01),
        'ssd_dt_bias': dt0 + jnp.log(-jnp.expm1(-dt0)),
        'ssd_a_log': jnp.log(jax.random.uniform(next(ks), (DEPTH, 2, H_SSD), f32, minval=1.0, maxval=16.0)),
        'ssd_d': 1.0 + nrm((DEPTH, H_SSD), 0.01),
        'ssd_norm_w': 1.0 + nrm((DEPTH, D_INNER), 0.01),
        'ssd_proj': nrm((DEPTH, D_INNER, D_MODEL), D_INNER ** -0.5),
        'ret_decay': ret_base + nrm((DEPTH, 2, H_RET), 0.01),
        'ret_gn_w': 1.0 + nrm((DEPTH, W_RET), 0.01),
        'ret_proj': nrm((DEPTH, W_RET, D_MODEL), W_RET ** -0.5),
        'w_out': nrm((DEPTH, D_MODEL, D_MODEL), BETA * D_MODEL ** -0.5),
        'ln1_g': 1.0 + nrm((DEPTH, D_MODEL), 0.01),
        'ln1_b': nrm((DEPTH, D_MODEL), 0.01),
        'ln2_g': 1.0 + nrm((DEPTH, D_MODEL), 0.01),
        'ln2_b': nrm((DEPTH, D_MODEL), 0.01),
        'router_w': nrm((DEPTH, D_MODEL, N_EXPERTS), D_MODEL ** -0.5),
        'router_b': nrm((DEPTH, N_EXPERTS), 0.01),
        'exp_w1': nrm((DEPTH, N_EXPERTS, D_MODEL, D_EXPERT), D_MODEL ** -0.5),
        'exp_w3': nrm((DEPTH, N_EXPERTS, D_MODEL, D_EXPERT), D_MODEL ** -0.5),
        'exp_w2': nrm((DEPTH, N_EXPERTS, D_EXPERT, D_MODEL), BETA * D_EXPERT ** -0.5),
    }


def reference(x_prompt, x_sample, c, state_ssd, state_ret, c_ctx, w_in, ada_w, ada_b,
              hy_conv_w, hy_conv_b, hy_w1, hy_b1, hy_w2, hy_b2, hy_w3, hy_freq, hy_bias, hy_proj,
              ssd_conv_w, ssd_conv_b, ssd_dt_bias, ssd_a_log, ssd_d, ssd_norm_w, ssd_proj,
              ret_decay, ret_gn_w, ret_proj, w_out, ln1_g, ln1_b, ln2_g, ln2_b,
              router_w, router_b, exp_w1, exp_w3, exp_w2):
    P = dict(w_in=w_in, ada_w=ada_w, ada_b=ada_b, hy_conv_w=hy_conv_w, hy_conv_b=hy_conv_b,
             hy_w1=hy_w1, hy_b1=hy_b1, hy_w2=hy_w2, hy_b2=hy_b2, hy_w3=hy_w3, hy_freq=hy_freq,
             hy_bias=hy_bias, hy_proj=hy_proj, ssd_conv_w=ssd_conv_w, ssd_conv_b=ssd_conv_b,
             ssd_dt_bias=ssd_dt_bias, ssd_a_log=ssd_a_log, ssd_d=ssd_d, ssd_norm_w=ssd_norm_w,
             ssd_proj=ssd_proj, ret_decay=ret_decay, ret_gn_w=ret_gn_w, ret_proj=ret_proj,
             w_out=w_out, ln1_g=ln1_g, ln1_b=ln1_b, ln2_g=ln2_g, ln2_b=ln2_b,
             router_w=router_w, router_b=router_b, exp_w1=exp_w1, exp_w3=exp_w3, exp_w2=exp_w2)

    bp = x_prompt.shape[0]
    zero_ssd = jnp.zeros((bp, 2, H_SSD, P_SSD, N_SSD), jnp.float32)
    zero_ret = jnp.zeros((bp, 2, H_RET, DV_RET, DK_RET), jnp.float32)
    y_prompt = x_prompt
    ssd_states, ret_states = [], []
    for l in range(DEPTH):
        y_prompt, s_ssd, s_ret = trunk_layer(y_prompt, c_ctx[None], P, l, None, zero_ssd, zero_ret)
        ssd_states.append(s_ssd)
        ret_states.append(s_ret)
    new_state_ssd = jnp.stack(ssd_states, axis=1)
    new_state_ret = jnp.stack(ret_states, axis=1)

    n_lat = x_sample.shape[1]
    rows = n_lat // GRID_W
    row_pos = jnp.repeat(jnp.arange(rows, dtype=jnp.int32), GRID_W)
    col_pos = jnp.tile(jnp.arange(GRID_W, dtype=jnp.int32), rows)
    y_sample = x_sample
    for l in range(DEPTH):
        y_sample, _, _ = trunk_layer(y_sample, c, P, l, (row_pos, col_pos), state_ssd[:, l], state_ret[:, l])

    return (y_prompt, y_sample, new_state_ssd, new_state_ret)
```

```python
import functools
import math

import jax
import jax.numpy as jnp
from jax import lax
from jax.experimental import pallas as pl
from jax.experimental.pallas import tpu as pltpu

F32 = jnp.float32
BF16 = jnp.bfloat16
HIGHEST = lax.Precision.HIGHEST

D = 1024
DEPTH = 4
GRID_W = 64
LN_EPS = 1e-5
W_HY = D
POS_EMB = 33
FILT_HID = 64
HY_TARGET, HY_FAST, HY_SLOW = 1e-2, 0.3, 1.5
D_INNER = 2 * D
P_SSD = 64
H_SSD = D_INNER // P_SSD
G_SSD = 4
N_SSD = 128
R_SSD = H_SSD // G_SSD
CONV_CH = D_INNER + 2 * G_SSD * N_SSD
H_RET = 4
DK_RET = 256
DV_RET = 256
ROPE_BASE = 10000.0
N_EXPERTS = 16
EC_CAPACITY = 2
ALPHA = (2 * DEPTH) ** 0.25

OFF_HY, OFF_Z, OFF_XBC, OFF_Q, OFF_K, OFF_V, OFF_G, OFF_MG, OFF_DT = (
    0, 3072, 5120, 8192, 9216, 10240, 11264, 12288, 15360)
P_PAD = 15872

TB = 256
V7X_VMEM_LIMIT = 56 * 1024 * 1024


def _cp(sem, vmem=None):
    return pltpu.CompilerParams(dimension_semantics=sem, vmem_limit_bytes=vmem or V7X_VMEM_LIMIT)


def _silu(x):
    return x * jax.nn.sigmoid(x)


def _ln(x):
    mu = jnp.mean(x, axis=-1, keepdims=True)
    xc = x - mu
    var = jnp.mean(xc * xc, axis=-1, keepdims=True)
    return xc * lax.rsqrt(var + LN_EPS)


def _split3(a):
    hi = a.astype(BF16)
    r1 = a - hi.astype(F32)
    mid = r1.astype(BF16)
    lo = (r1 - mid.astype(F32)).astype(BF16)
    return hi, mid, lo


def _dot(a, b):
    return jnp.dot(a, b, preferred_element_type=F32)


def _dot3_l(a_f32, b_bf16):
    hi, mid, lo = _split3(a_f32)
    return _dot(hi, b_bf16) + _dot(mid, b_bf16) + _dot(lo, b_bf16)


def _dot3_r(a_bf16, b_f32):
    hi, mid, lo = _split3(b_f32)
    return _dot(a_bf16, hi) + _dot(a_bf16, mid) + _dot(a_bf16, lo)


class Geom:
    def __init__(self, bp, lp, bs, ls):
        self.bp, self.lp, self.bs, self.ls = bp, lp, bs, ls
        self.tp, self.ts = bp * lp, bs * ls
        self.t = self.tp + self.ts
        self.tpp, self.tps = lp // TB, ls // TB
        self.npt, self.nst = self.tp // TB, self.ts // TB
        self.nt = self.npt + self.nst

    def mod_row(self, i, rows_per_tile=TB):
        npt = self.tp // rows_per_tile
        per = self.ls // rows_per_tile
        return jnp.where(i < npt, 0, 1 + (i - npt) // per)


def _ada_kernel(c_ref, w_ref, b_ref, o_ref):
    s = _silu(c_ref[...])
    o_ref[...] = jnp.dot(s, w_ref[...], precision=HIGHEST, preferred_element_type=F32) + b_ref[...]


def ada_mod(cond16, ada_w, ada_b):
    tn = 1536
    return pl.pallas_call(
        _ada_kernel,
        out_shape=jax.ShapeDtypeStruct((DEPTH, 16, 6 * D), F32),
        grid=(DEPTH, 6 * D // tn),
        in_specs=[pl.BlockSpec((16, D), lambda l, j: (0, 0)),
                  pl.BlockSpec((None, D, tn), lambda l, j: (l, 0, j)),
                  pl.BlockSpec((None, 1, tn), lambda l, j: (l, 0, j))],
        out_specs=pl.BlockSpec((None, 16, tn), lambda l, j: (l, 0, j)),
        compiler_params=_cp(("parallel", "parallel")),
        name="ada_mod",
    )(cond16, ada_w, ada_b.reshape(DEPTH, 1, 6 * D))


def _inproj_kernel(x_ref, sh_ref, sc_ref, w_ref, o_ref, h_ref):
    @pl.when(pl.program_id(1) == 0)
    def _():
        h = _ln(x_ref[...]) * (1.0 + sc_ref[...]) + sh_ref[...]
        h_ref[...] = h.astype(BF16)

    o_ref[...] = _dot(h_ref[...], w_ref[...])


def in_proj(geom, x, mod, l, w_in_p):
    tm = next(t for t in (1024, 512, 256) if geom.tp % t == 0 and geom.ls % t == 0)
    tn = 512
    row = functools.partial(geom.mod_row, rows_per_tile=tm)
    return pl.pallas_call(
        _inproj_kernel,
        out_shape=jax.ShapeDtypeStruct((geom.t, P_PAD), F32),
        grid=(geom.t // tm, P_PAD // tn),
        in_specs=[pl.BlockSpec((tm, D), lambda i, j: (i, 0)),
                  pl.BlockSpec((None, None, None, 1, D), lambda i, j: (l, 0, row(i), 0, 0)),
                  pl.BlockSpec((None, None, None, 1, D), lambda i, j: (l, 1, row(i), 0, 0)),
                  pl.BlockSpec((None, D, tn), lambda i, j: (l, 0, j))],
        out_specs=pl.BlockSpec((tm, tn), lambda i, j: (i, j)),
        scratch_shapes=[pltpu.VMEM((tm, D), BF16)],
        compiler_params=_cp(("parallel", "arbitrary")),
        name="in_proj",
    )(x, mod, mod, w_in_p)


def _dwconv_kernel(geom, act, transpose, x_ref, xp_ref, xn_ref, w_ref, b_ref, o_ref):
    i = pl.program_id(0)
    pos = jnp.where(i < geom.npt, i % geom.tpp, (i - geom.npt) % geom.tps)
    last = jnp.where(i < geom.npt, geom.tpp - 1, geom.tps - 1)
    x = x_ref[...]
    rows = lax.broadcasted_iota(jnp.int32, x.shape, 0)
    prev_row = jnp.where(pos == 0, 0.0, xp_ref[7:8, :])
    next_row = jnp.where(pos == last, 0.0, xn_ref[0:1, :])
    x_prev = jnp.where(rows == 0, prev_row, pltpu.roll(x, 1, 0))
    x_next = jnp.where(rows == TB - 1, next_row, pltpu.roll(x, TB - 1, 0))
    w = w_ref[...]
    y = x_prev * w[0:1, :] + x * w[1:2, :] + x_next * w[2:3, :] + b_ref[...]
    if act:
        y = _silu(y)
    if transpose:
        y = y.T
    o_ref[...] = y.astype(o_ref.dtype)


def dwconv3(geom, u, col0, w, b, wcol0, ncols, *, act, out_dtype=F32, transpose=False, cw=512):
    nb8 = geom.t // 8
    c0, w0, nj = col0 // cw, wcol0 // cw, ncols // cw
    if transpose:
        out_shape = jax.ShapeDtypeStruct((ncols, geom.t), out_dtype)
        out_spec = pl.BlockSpec((cw, TB), lambda i, j: (j, i))
    else:
        out_shape = jax.ShapeDtypeStruct((geom.t, ncols), out_dtype)
        out_spec = pl.BlockSpec((TB, cw), lambda i, j: (i, j))
    return pl.pallas_call(
        functools.partial(_dwconv_kernel, geom, act, transpose),
        out_shape=out_shape,
        grid=(geom.nt, nj),
        in_specs=[pl.BlockSpec((TB, cw), lambda i, j: (i, c0 + j)),
                  pl.BlockSpec((8, cw), lambda i, j: (jnp.maximum(i * (TB // 8) - 1, 0), c0 + j)),
                  pl.BlockSpec((8, cw), lambda i, j: (jnp.minimum((i + 1) * (TB // 8), nb8 - 1), c0 + j)),
                  pl.BlockSpec((3, cw), lambda i, j: (0, w0 + j)),
                  pl.BlockSpec((1, cw), lambda i, j: (0, w0 + j))],
        out_specs=out_spec,
        compiler_params=_cp(("parallel", "parallel")),
        name="dwconv3",
    )(u, u, u, w, b)


def _dt_kernel(u_ref, bias_ref, nega_ref, dt_ref, la_ref):
    dt = jax.nn.softplus(u_ref[...] + bias_ref[...])
    dt_ref[...] = dt
    la_ref[...] = dt * nega_ref[...]


def ssd_dt(geom, u, dt_bias128, neg_a128):
    tm = 512 if geom.t % 512 == 0 else TB
    return pl.pallas_call(
        _dt_kernel,
        out_shape=[jax.ShapeDtypeStruct((geom.t, 128), F32)] * 2,
        grid=(geom.t // tm,),
        in_specs=[pl.BlockSpec((tm, 128), lambda i: (i, OFF_DT // 128)),
                  pl.BlockSpec((1, 128), lambda i: (0, 0)),
                  pl.BlockSpec((1, 128), lambda i: (0, 0))],
        out_specs=[pl.BlockSpec((tm, 128), lambda i: (i, 0))] * 2,
        compiler_params=_cp(("parallel",)),
        name="ssd_dt",
    )(u, dt_bias128, neg_a128)


def _qk_kernel(q_ref, k_ref, cos_ref, sin_ref, qo_ref, kto_ref):
    cos, sin = cos_ref[...], sin_ref[...]

    def rope(x):
        halves = [pltpu.roll(x[:, a:a + 128], 64, 1) for a in (0, 128)]
        return x * cos + jnp.concatenate(halves, axis=1) * sin

    qo_ref[...] = rope(q_ref[...]).astype(BF16)
    kto_ref[...] = (rope(k_ref[...]) * (DK_RET ** -0.5)).T.astype(BF16)


def qk_prep(geom, u, cos_tab, sin_tab):
    tab = lambda i, h: (jnp.where(i < geom.npt, geom.tps, (i - geom.npt) % geom.tps), 0)
    return pl.pallas_call(
        _qk_kernel,
        out_shape=[jax.ShapeDtypeStruct((geom.t, H_RET * DK_RET), BF16),
                   jax.ShapeDtypeStruct((H_RET * DK_RET, geom.t), BF16)],
        grid=(geom.nt, H_RET),
        in_specs=[pl.BlockSpec((TB, DK_RET), lambda i, h: (i, OFF_Q // DK_RET + h)),
                  pl.BlockSpec((TB, DK_RET), lambda i, h: (i, OFF_K // DK_RET + h)),
                  pl.BlockSpec((TB, DK_RET), tab),
                  pl.BlockSpec((TB, DK_RET), tab)],
        out_specs=[pl.BlockSpec((TB, DK_RET), lambda i, h: (i, h)),
                   pl.BlockSpec((DK_RET, TB), lambda i, h: (h, i))],
        compiler_params=_cp(("parallel", "parallel")),
        name="qk_prep",
    )(u, u, cos_tab, sin_tab)


def _scan_kernel(r, p, n, nc, reverse, has_dt, has_init, *refs):
    it = iter(refs)
    x_ref, c_ref, bt_ref, la_ref, lat_ref = next(it), next(it), next(it), next(it), next(it)
    dt_ref = next(it) if has_dt else None
    s0_ref = next(it) if has_init else None
    y_ref, sf_ref, s_ref, yd_ref = next(it), next(it), next(it), next(it)
    c = pl.program_id(2)
    q = TB

    @pl.when(c == 0)
    def _():
        s_ref[...] = s0_ref[...] if has_init else jnp.zeros_like(s_ref)

    ii = lax.broadcasted_iota(jnp.int32, (q, q), 0)
    jj = lax.broadcasted_iota(jnp.int32, (q, q), 1)
    lower = ii >= jj
    tri_a = (jj >= ii) if reverse else lower
    tri_a_bf = tri_a.astype(BF16)
    tri_b_bf = (lower if reverse else (jj >= ii)).astype(BF16)
    end = 0 if reverse else q - 1

    la_col = la_ref[...]
    la_row = lat_ref[...]
    if r == 1:
        la_col = jnp.broadcast_to(la_col, (q, 128))
        la_row = jnp.broadcast_to(la_row, (8, q))
    cs_col = _dot3_r(tri_a_bf, la_col)[:, :r]
    cs_row = _dot3_l(la_row, tri_b_bf)[:r, :]
    tot = cs_col[end:end + 1, :]
    end_decay = jnp.exp(tot - cs_col)
    in_scale = jnp.exp(cs_col)

    if r == 1:
        expand = lambda a: jnp.broadcast_to(a, (q, p))
    else:
        er = lax.broadcasted_iota(jnp.int32, (r, r * p), 0)
        ec = lax.broadcasted_iota(jnp.int32, (r, r * p), 1)
        e_bf = (ec // p == er).astype(BF16)
        expand = lambda a: _dot3_l(a, e_bf)

    x = x_ref[...].astype(F32)
    if has_dt:
        dt = dt_ref[...]
        xd = x * expand(dt)
        xe = x * expand(dt * end_decay)
    else:
        xd = x
        xe = x * expand(end_decay)
    xd_bf, xe_bf = xd.astype(BF16), xe.astype(BF16)
    c_bf = c_ref[...].astype(BF16)
    bt_bf = bt_ref[...].astype(BF16)

    scores = _dot(c_bf, bt_bf)
    for h in range(r):
        dmat = cs_col[:, h:h + 1] - cs_row[h:h + 1, :]
        wgt = scores * jnp.exp(jnp.where(tri_a, dmat, -1e30))
        yd_ref[:, h * p:(h + 1) * p] = _dot(wgt.astype(BF16), xd_bf[:, h * p:(h + 1) * p])

    in_scale_x = expand(in_scale)
    s_old = s_ref[...]
    y_ref[...] = yd_ref[...] + _dot(c_bf, s_old.astype(BF16)) * in_scale_x
    s_new = s_old * in_scale_x[end:end + 1, :] + _dot(bt_bf, xe_bf)
    s_ref[...] = s_new

    @pl.when(c == nc - 1)
    def _():
        sf_ref[...] = s_new


def chunk_scan(x, xcol0, cm, ccol0, bt, la, lat, dt, s0, *, tile0, nseq, nc, g, r, p, n, reverse):
    rp = r * p
    xb, cb = xcol0 // rp, ccol0 // n
    ch = (lambda c: nc - 1 - c) if reverse else (lambda c: c)
    tok = lambda b, c: tile0 + b * nc + ch(c)
    in_specs = [pl.BlockSpec((TB, rp), lambda b, gi, c: (tok(b, c), xb + gi)),
                pl.BlockSpec((TB, n), lambda b, gi, c: (tok(b, c), cb + gi)),
                pl.BlockSpec((n, TB), lambda b, gi, c: (gi, tok(b, c))),
                pl.BlockSpec((None, TB, r), lambda b, gi, c: (gi, tok(b, c), 0)),
                pl.BlockSpec((None, r, TB), lambda b, gi, c: (gi, 0, tok(b, c)))]
    args = [x, cm, bt, la, lat]
    if dt is not None:
        in_specs.append(pl.BlockSpec((None, TB, r), lambda b, gi, c: (gi, tok(b, c), 0)))
        args.append(dt)
    if s0 is not None:
        in_specs.append(pl.BlockSpec((None, None, n, rp), lambda b, gi, c: (b, gi, 0, 0)))
        args.append(s0)
    return pl.pallas_call(
        functools.partial(_scan_kernel, r, p, n, nc, reverse, dt is not None, s0 is not None),
        out_shape=[jax.ShapeDtypeStruct((nseq * nc * TB, g * rp), F32),
                   jax.ShapeDtypeStruct((nseq, g, n, rp), F32)],
        grid=(nseq, g, nc),
        in_specs=in_specs,
        out_specs=[pl.BlockSpec((TB, rp), lambda b, gi, c: (b * nc + ch(c), gi)),
                   pl.BlockSpec((None, None, n, rp), lambda b, gi, c: (b, gi, 0, 0))],
        scratch_shapes=[pltpu.VMEM((n, rp), F32), pltpu.VMEM((TB, rp), F32)],
        compiler_params=_cp(("parallel", "parallel", "arbitrary")),
        name="chunk_scan",
    )(*args)


def _filt_hidden_kernel(ls, freq_ref, w1a_ref, w1c_ref, w1s_ref, b1_ref, w2t_ref, b2_ref, sf_ref, o_ref):
    pp = lax.broadcasted_iota(jnp.int32, (1, 2 * ls), 1)
    t = jnp.abs(pp - ls).astype(F32)
    t_norm = t / max(ls - 1, 1)
    ang = (2.0 * math.pi) * t * freq_ref[...] / ls
    pre = (w1a_ref[...] * t_norm
           + jnp.dot(w1c_ref[...], jnp.cos(ang), precision=HIGHEST, preferred_element_type=F32)
           - jnp.dot(w1s_ref[...], jnp.sin(ang), precision=HIGHEST, preferred_element_type=F32)
           + b1_ref[...])
    sf = sf_ref[...]
    h1 = jnp.sin(sf[:, 0:1] * pre)
    h2 = jnp.sin(sf[:, 1:2] * (jnp.dot(w2t_ref[...], h1, precision=HIGHEST, preferred_element_type=F32)
                               + b2_ref[...]))
    o_ref[...] = h2


def _filt_out_kernel(ls, hid_ref, wf_ref, wb_ref, delta_ref, bias_ref, o_ref):
    pp = lax.broadcasted_iota(jnp.int32, (1, 2 * ls), 1)
    t_norm = jnp.abs(pp - ls).astype(F32) / max(ls - 1, 1)
    hid = hid_ref[...]
    kf = jnp.dot(wf_ref[...], hid, precision=HIGHEST, preferred_element_type=F32)
    kb = jnp.dot(wb_ref[...], hid, precision=HIGHEST, preferred_element_type=F32)
    window = jnp.exp(-t_norm * jnp.abs(delta_ref[...]))
    k = jnp.where(pp >= ls, kf, kb) * window
    k = jnp.where(pp == 0, 0.0, k)
    o_ref[...] = k + jnp.where(pp == ls, bias_ref[...], 0.0)


def hyena_filter_rows(ls, hy_w1, hy_b1, hy_w2, hy_b2, hy_w3, hy_freq, hy_bias):
    bands = (POS_EMB - 1) // 2
    freqs = jnp.linspace(1e-4, bands - 1, bands, dtype=F32).reshape(bands, 1)
    w1t = jnp.swapaxes(hy_w1, 1, 2)
    hid = pl.pallas_call(
        functools.partial(_filt_hidden_kernel, ls),
        out_shape=jax.ShapeDtypeStruct((DEPTH, FILT_HID, 2 * ls), F32),
        grid=(DEPTH,),
        in_specs=[pl.BlockSpec((bands, 1), lambda l: (0, 0)),
                  pl.BlockSpec((None, FILT_HID, 1), lambda l: (l, 0, 0)),
                  pl.BlockSpec((None, FILT_HID, bands), lambda l: (l, 0, 0)),
                  pl.BlockSpec((None, FILT_HID, bands), lambda l: (l, 0, 0)),
                  pl.BlockSpec((None, FILT_HID, 1), lambda l: (l, 0, 0)),
                  pl.BlockSpec((None, FILT_HID, FILT_HID), lambda l: (l, 0, 0)),
                  pl.BlockSpec((None, FILT_HID, 1), lambda l: (l, 0, 0)),
                  pl.BlockSpec((None, FILT_HID, 2), lambda l: (l, 0, 0))],
        out_specs=pl.BlockSpec((None, FILT_HID, 2 * ls), lambda l: (l, 0, 0)),
        compiler_params=_cp(("parallel",)),
        name="hyena_filter_hidden",
    )(freqs, w1t[:, :, 0:1], w1t[:, :, 1:1 + bands], w1t[:, :, 1 + bands:], hy_b1[..., None],
      jnp.swapaxes(hy_w2, 1, 2), hy_b2[..., None], jnp.swapaxes(hy_freq, 1, 2))
    w3t = jnp.swapaxes(hy_w3, 1, 2).reshape(DEPTH, 2, 2, W_HY, FILT_HID)
    deltas = jnp.linspace(math.log(HY_TARGET) / HY_SLOW, math.log(HY_TARGET) / HY_FAST, W_HY,
                          dtype=F32).reshape(W_HY, 1)
    cb = 256
    return pl.pallas_call(
        functools.partial(_filt_out_kernel, ls),
        out_shape=jax.ShapeDtypeStruct((DEPTH, 2, W_HY, 2 * ls), F32),
        grid=(DEPTH, 2, W_HY // cb),
        in_specs=[pl.BlockSpec((None, FILT_HID, 2 * ls), lambda l, o, j: (l, 0, 0)),
                  pl.BlockSpec((None, None, None, cb, FILT_HID), lambda l, o, j: (l, o, 0, j, 0)),
                  pl.BlockSpec((None, None, None, cb, FILT_HID), lambda l, o, j: (l, o, 1, j, 0)),
                  pl.BlockSpec((cb, 1), lambda l, o, j: (j, 0)),
                  pl.BlockSpec((None, None, cb, 1), lambda l, o, j: (l, o, j, 0))],
        out_specs=pl.BlockSpec((None, None, cb, 2 * ls), lambda l, o, j: (l, o, j, 0)),
        compiler_params=_cp(("parallel", "parallel", "parallel")),
        name="hyena_filter_rows",
    )(hid, w3t, w3t, deltas, hy_bias[..., None])


def _hyena_kernel(nb, b, cbk, v_ref, x1_ref, x2_ref, w_ref, o_ref, acc_ref):
    ls2 = 2 * nb * TB

    def conv(u, o, ch):
        wrow = w_ref[o, pl.ds(ch, 1), :]
        toep = pltpu.roll(jnp.broadcast_to(wrow, (TB, ls2)), 0, 1, stride=1, stride_axis=0).astype(BF16)
        acc_ref[...] = jnp.zeros_like(acc_ref)
        for mm in range(2 * nb - 1):
            m = mm - (nb - 1)
            lo, hi = max(0, -m), min(nb, nb - m)
            part = _dot(u[lo * b:hi * b, :].astype(BF16), toep[:, TB * (mm + 1):TB * (mm + 2)])
            acc_ref[(lo + m) * b:(hi + m) * b, :] += part
        return acc_ref[...]

    def body(ch, carry):
        z = x1_ref[ch] * conv(v_ref[ch], 0, ch)
        o_ref[ch] = x2_ref[ch] * conv(z, 1, ch)
        return carry

    lax.fori_loop(0, cbk, body, 0)


def hyena_conv(uct, filt_l, nb, b):
    cbk = 8
    rows = nb * b
    return pl.pallas_call(
        functools.partial(_hyena_kernel, nb, b, cbk),
        out_shape=jax.ShapeDtypeStruct((W_HY, rows, TB), F32),
        grid=(W_HY // cbk,),
        in_specs=[pl.BlockSpec((None, cbk, rows, TB), lambda j: (0, j, 0, 0)),
                  pl.BlockSpec((None, cbk, rows, TB), lambda j: (1, j, 0, 0)),
                  pl.BlockSpec((None, cbk, rows, TB), lambda j: (2, j, 0, 0)),
                  pl.BlockSpec((2, cbk, 2 * nb * TB), lambda j: (0, j, 0))],
        out_specs=pl.BlockSpec((cbk, rows, TB), lambda j: (j, 0, 0)),
        scratch_shapes=[pltpu.VMEM((rows, TB), F32)],
        compiler_params=_cp(("parallel",)),
        name="hyena_conv",
    )(uct, uct, uct, filt_l)


def _ssd_post_kernel(yf_ref, yb_ref, xs_ref, z_ref, d_ref, nw_ref, o_ref):
    y = yf_ref[...] + yb_ref[...] + xs_ref[...] * d_ref[...]
    y = y * _silu(z_ref[...])
    y = y * lax.rsqrt(jnp.mean(y * y, axis=-1, keepdims=True) + LN_EPS)
    o_ref[...] = (y * nw_ref[...]).astype(o_ref.dtype)


def ssd_post(geom, yf, yb, xs, u, d_exp, norm_w):
    gw = D_INNER // G_SSD
    tm = 512 if geom.t % 512 == 0 else TB
    blk = pl.BlockSpec((tm, gw), lambda i, gi: (i, gi))
    vec = pl.BlockSpec((1, gw), lambda i, gi: (0, gi))
    return pl.pallas_call(
        _ssd_post_kernel,
        out_shape=jax.ShapeDtypeStruct((geom.t, D_INNER), BF16),
        grid=(geom.t // tm, G_SSD),
        in_specs=[blk, blk, blk, pl.BlockSpec((tm, gw), lambda i, gi: (i, OFF_Z // gw + gi)), vec, vec],
        out_specs=blk,
        compiler_params=_cp(("parallel", "parallel")),
        name="ssd_post",
    )(yf, yb, xs, u, d_exp, norm_w)


def _ret_post_kernel(of_ref, ob_ref, g_ref, w_ref, o_ref):
    o = _ln(of_ref[...] + ob_ref[...]) * w_ref[...]
    o_ref[...] = (o * _silu(g_ref[...])).astype(o_ref.dtype)


def ret_post(geom, of, ob, u, gn_w):
    tm = 512 if geom.t % 512 == 0 else TB
    blk = pl.BlockSpec((tm, DV_RET), lambda i, h: (i, h))
    return pl.pallas_call(
        _ret_post_kernel,
        out_shape=jax.ShapeDtypeStruct((geom.t, H_RET * DV_RET), BF16),
        grid=(geom.t // tm, H_RET),
        in_specs=[blk, blk, pl.BlockSpec((tm, DV_RET), lambda i, h: (i, OFF_G // DV_RET + h)),
                  pl.BlockSpec((1, DV_RET), lambda i, h: (0, h))],
        out_specs=blk,
        compiler_params=_cp(("parallel", "parallel")),
        name="ret_post",
    )(of, ob, u, gn_w)


def _merge_kernel(yhy_ref, yssd_ref, yret_ref, mg0_ref, mg1_ref, mg2_ref, x_ref, g1_ref, sh2_ref, sc2_ref,
                  lg_ref, lb_ref, whp_ref, wsp_ref, wrp_ref, wo_ref, rwt_ref, rb_ref,
                  x1_ref, h_ref, aff_ref):
    merged = (jax.nn.sigmoid(mg0_ref[...]) * _dot(yhy_ref[...].astype(BF16), whp_ref[...])
              + jax.nn.sigmoid(mg1_ref[...]) * _dot(yssd_ref[...], wsp_ref[...])
              + jax.nn.sigmoid(mg2_ref[...]) * _dot(yret_ref[...], wrp_ref[...]))
    mix = _dot(merged.astype(BF16), wo_ref[...])
    x1 = _ln(ALPHA * x_ref[...] + g1_ref[...] * mix) * lg_ref[...] + lb_ref[...]
    x1_ref[...] = x1
    h = _ln(x1) * (1.0 + sc2_ref[...]) + sh2_ref[...]
    h_ref[...] = h.astype(BF16)
    logits = lax.dot_general(rwt_ref[...], h, (((1,), (1,)), ((), ())), precision=HIGHEST,
                             preferred_element_type=F32) + rb_ref[...]
    e = jnp.exp(logits - jnp.max(logits, axis=0, keepdims=True))
    aff_ref[...] = e / jnp.sum(e, axis=0, keepdims=True)


def merge(geom, l, y_hy, y_ssd, y_ret, u, x, mod, ln_g, ln_b, whp, wsp, wrp, wo, rwt, rb):
    tm = TB
    row = geom.mod_row
    tok = lambda w: pl.BlockSpec((tm, w), lambda i: (i, 0))
    mgs = [pl.BlockSpec((tm, D), lambda i, k=k: (i, OFF_MG // D + k)) for k in range(3)]
    modspec = lambda k: pl.BlockSpec((None, None, None, 1, D), lambda i: (l, k, row(i), 0, 0))
    vec = pl.BlockSpec((None, 1, D), lambda i: (l, 0, 0))
    wspec = lambda kk: pl.BlockSpec((None, kk, D), lambda i: (l, 0, 0))
    return pl.pallas_call(
        _merge_kernel,
        out_shape=[jax.ShapeDtypeStruct((geom.t, D), F32), jax.ShapeDtypeStruct((geom.t, D), BF16),
                   jax.ShapeDtypeStruct((N_EXPERTS, geom.t), F32)],
        grid=(geom.t // tm,),
        in_specs=[tok(D), tok(D_INNER), tok(D)] + mgs + [tok(D), modspec(2), modspec(3), modspec(4), vec, vec,
                  wspec(D), wspec(D_INNER), wspec(D), wspec(D),
                  pl.BlockSpec((None, N_EXPERTS, D), lambda i: (l, 0, 0)),
                  pl.BlockSpec((None, N_EXPERTS, 1), lambda i: (l, 0, 0))],
        out_specs=[tok(D), tok(D), pl.BlockSpec((N_EXPERTS, tm), lambda i: (0, i))],
        compiler_params=_cp(("parallel",)),
        name="merge",
    )(y_hy, y_ssd, y_ret, u, u, u, x, mod, mod, mod, ln_g, ln_b, whp, wsp, wrp, wo, rwt, rb)


def _select_kernel(n, cap, aff_ref, slot_ref):
    aff = aff_ref[...]
    bits = pltpu.bitcast(aff, jnp.int32)

    def bit_step(k, thr):
        cand = thr | jnp.left_shift(jnp.int32(1), 30 - k)
        cnt = jnp.sum(jnp.where(bits >= cand, 1.0, 0.0), axis=1, keepdims=True)
        return jnp.where(cnt >= cap, cand, thr)

    thr = lax.fori_loop(0, 31, bit_step, jnp.zeros((N_EXPERTS, 1), jnp.int32))
    gt = bits > thr
    eq = bits == thr
    need = cap - jnp.sum(jnp.where(gt, 1.0, 0.0), axis=1, keepdims=True)

    kk = lax.broadcasted_iota(jnp.int32, (TB, TB), 0)
    jj = lax.broadcasted_iota(jnp.int32, (TB, TB), 1)
    before = (kk < jj).astype(BF16)

    def excl_cumsum(mask):
        ones = jnp.where(mask, 1.0, 0.0)
        outs, carry = [], jnp.zeros((N_EXPERTS, 1), F32)
        for j in range(n // TB):
            blk = ones[:, j * TB:(j + 1) * TB]
            outs.append(_dot(blk.astype(BF16), before) + carry)
            carry = carry + jnp.sum(blk, axis=1, keepdims=True)
        return jnp.concatenate(outs, axis=1)

    sel = gt | (eq & (excl_cumsum(eq) < need))
    slot_ref[...] = jnp.where(sel, excl_cumsum(sel), -1.0)


def ec_select(aff_t, tile0, nseq, n):
    cap = EC_CAPACITY * n // N_EXPERTS
    nt = n // TB
    return pl.pallas_call(
        functools.partial(_select_kernel, n, cap),
        out_shape=jax.ShapeDtypeStruct((N_EXPERTS, nseq * n), F32),
        grid=(nseq,),
        in_specs=[pl.BlockSpec((N_EXPERTS, n), lambda s: (0, tile0 // nt + s))],
        out_specs=pl.BlockSpec((N_EXPERTS, n), lambda s: (0, s)),
        compiler_params=_cp(("parallel",)),
        name="ec_select",
    )(aff_t)


def _moe_kernel(n, cap, slot_ref, aff_ref, h_ref, w1_ref, w3_ref, w2_ref, o_ref):
    e = pl.program_id(1)

    @pl.when(e == 0)
    def _():
        o_ref[...] = jnp.zeros_like(o_ref)

    slot = slot_ref[pl.ds(e, 1), :]
    gate = aff_ref[pl.ds(e, 1), :]
    sc = min(cap, 256)
    for j0 in range(0, cap, sc):
        jcol = (lax.broadcasted_iota(jnp.int32, (sc, 1), 0) + j0).astype(F32)
        pick = slot == jcol
        pick_bf = jnp.where(pick, 1.0, 0.0).astype(BF16)
        xe = _dot(pick_bf, h_ref[...]).astype(BF16)
        g_slot = jnp.sum(jnp.where(pick, gate, 0.0), axis=1, keepdims=True)
        hid = _silu(_dot(xe, w1_ref[...])) * _dot(xe, w3_ref[...]) * g_slot
        ye = _dot(hid.astype(BF16), w2_ref[...])
        ye_hi = ye.astype(BF16)
        ye_lo = (ye - ye_hi.astype(F32)).astype(BF16)
        dn = (((0,), (0,)), ((), ()))
        o_ref[...] += (lax.dot_general(pick_bf, ye_hi, dn, preferred_element_type=F32)
                       + lax.dot_general(pick_bf, ye_lo, dn, preferred_element_type=F32))


def moe_experts(l, slot, aff_t, h, w1, w3, w2, tile0, nseq, n):
    cap = EC_CAPACITY * n // N_EXPERTS
    nt = n // TB
    s0 = tile0 // nt
    wspec = pl.BlockSpec((None, None, D, D), lambda s, e: (l, e, 0, 0))
    return pl.pallas_call(
        functools.partial(_moe_kernel, n, cap),
        out_shape=jax.ShapeDtypeStruct((nseq * n, D), F32),
        grid=(nseq, N_EXPERTS),
        in_specs=[pl.BlockSpec((N_EXPERTS, n), lambda s, e: (0, s)),
                  pl.BlockSpec((N_EXPERTS, n), lambda s, e: (0, s0 + s)),
                  pl.BlockSpec((n, D), lambda s, e: (s0 + s, 0), pipeline_mode=pl.Buffered(1)),
                  wspec, wspec, wspec],
        out_specs=pl.BlockSpec((n, D), lambda s, e: (s, 0), pipeline_mode=pl.Buffered(1)),
        compiler_params=_cp(("parallel", "arbitrary"), 60 * 1024 * 1024),
        name="moe_experts",
    )(slot, aff_t, h, w1, w3, w2)


def _resid_kernel(x_ref, f_ref, g_ref, lg_ref, lb_ref, o_ref):
    o_ref[...] = _ln(ALPHA * x_ref[...] + g_ref[...] * f_ref[...]) * lg_ref[...] + lb_ref[...]


def resid_ln(geom, l, x1, ffn, mod, ln_g, ln_b):
    tm = 512 if (geom.tp % 512 == 0 and geom.ls % 512 == 0) else TB
    row = functools.partial(geom.mod_row, rows_per_tile=tm)
    tok = pl.BlockSpec((tm, D), lambda i: (i, 0))
    vec = pl.BlockSpec((None, 1, D), lambda i: (l, 0, 0))
    return pl.pallas_call(
        _resid_kernel,
        out_shape=jax.ShapeDtypeStruct((geom.t, D), F32),
        grid=(geom.t // tm,),
        in_specs=[tok, tok, pl.BlockSpec((None, None, None, 1, D), lambda i: (l, 5, row(i), 0, 0)), vec, vec],
        out_specs=tok,
        compiler_params=_cp(("parallel",)),
        name="resid_ln",
    )(x1, ffn, mod, ln_g, ln_b)


def _rope_tables(ls):
    quarter = DK_RET // 4
    inv_freq = ROPE_BASE ** (-jnp.arange(quarter, dtype=F32) / quarter)
    pos = jnp.arange(ls, dtype=jnp.int32)
    row_ang = (pos // GRID_W).astype(F32)[:, None] * inv_freq[None]
    col_ang = (pos % GRID_W).astype(F32)[:, None] * inv_freq[None]
    cos = jnp.concatenate([jnp.cos(row_ang)] * 2 + [jnp.cos(col_ang)] * 2, axis=1)
    sin = jnp.concatenate([-jnp.sin(row_ang), jnp.sin(row_ang), -jnp.sin(col_ang), jnp.sin(col_ang)], axis=1)
    cos = jnp.concatenate([cos, jnp.ones((TB, DK_RET), F32)], axis=0)
    sin = jnp.concatenate([sin, jnp.zeros((TB, DK_RET), F32)], axis=0)
    return cos, sin


def _to_channel_major(uc, b, nb):
    x = uc.reshape(b, nb, TB, 3, W_HY)
    return jnp.transpose(x, (3, 4, 1, 0, 2)).reshape(3, W_HY, nb * b, TB)


def _to_token_major(yt, b, nb):
    x = yt.reshape(W_HY, nb, b, TB)
    return jnp.transpose(x, (2, 1, 3, 0)).reshape(b * nb * TB, W_HY)


def _state_to_t(s, g, r):
    b, h, p, n = s.shape
    return jnp.transpose(s.reshape(b, g, r, p, n), (0, 1, 4, 2, 3)).reshape(b, g, n, r * p)


def _state_from_t(st, r):
    b, g, n, rp = st.shape
    p = rp // r
    return jnp.transpose(st.reshape(b, g, n, r, p), (0, 1, 3, 4, 2)).reshape(b, g * r, p, n)


def _per_group(a, g, r):
    t = a.shape[0]
    a3 = a.reshape(t, g, r)
    return jnp.transpose(a3, (1, 0, 2)), jnp.transpose(a3, (1, 2, 0))


def kernel(x_prompt, x_sample, c, state_ssd, state_ret, c_ctx, w_in, ada_w, ada_b, hy_conv_w, hy_conv_b, hy_w1, hy_b1, hy_w2, hy_b2, hy_w3, hy_freq, hy_bias, hy_proj, ssd_conv_w, ssd_conv_b, ssd_dt_bias, ssd_a_log, ssd_d, ssd_norm_w, ssd_proj, ret_decay, ret_gn_w, ret_proj, w_out, ln1_g, ln1_b, ln2_g, ln2_b, router_w, router_b, exp_w1, exp_w3, exp_w2):
    bp, lp, _ = x_prompt.shape
    bs, ls, _ = x_sample.shape
    geom = Geom(bp, lp, bs, ls)
    assert lp % TB == 0 and ls % TB == 0 and bs + 1 <= 16 and bp % 8 == 0 and bs % 8 == 0 and (bp * lp) % ls == 0

    w_in_p = jnp.concatenate([w_in[:, :, :OFF_Q], w_in[:, :, OFF_Q + 64:], w_in[:, :, OFF_Q:OFF_Q + 64],
                              jnp.zeros((DEPTH, D, P_PAD - OFF_DT - 64), F32)], axis=2).astype(BF16)
    whp, wsp, wrp, wo = (w.astype(BF16) for w in (hy_proj, ssd_proj, ret_proj, w_out))
    ew1, ew3, ew2 = (w.astype(BF16) for w in (exp_w1, exp_w3, exp_w2))
    rwt = jnp.swapaxes(router_w, 1, 2)
    rb = router_b[..., None]
    vec3 = lambda a: a[:, None, :]
    cond16 = jnp.zeros((16, D), F32).at[0].set(c_ctx).at[1:1 + bs].set(c)
    mod = ada_mod(cond16, ada_w, ada_b)
    mod = jnp.transpose(mod.reshape(DEPTH, 16, 6, D), (0, 2, 1, 3))[:, :, :, None, :]

    filt_p = hyena_filter_rows(lp, hy_w1, hy_b1, hy_w2, hy_b2, hy_w3, hy_freq, hy_bias)
    filt_s = hyena_filter_rows(ls, hy_w1, hy_b1, hy_w2, hy_b2, hy_w3, hy_freq, hy_bias)
    cos_tab, sin_tab = _rope_tables(ls)
    pad64 = lambda a: jnp.pad(a.reshape(1, 2 * H_SSD), ((0, 0), (0, 128 - 2 * H_SSD)))
    d_exp = jnp.repeat(ssd_d, P_SSD, axis=1)
    ret_la = -jnp.exp(ret_decay)

    x = jnp.concatenate([x_prompt.reshape(geom.tp, D), x_sample.reshape(geom.ts, D)], axis=0)
    ssd_states, ret_states = [], []
    groups = ((0, bp, geom.tpp), (geom.npt, bs, geom.tps))

    for l in range(DEPTH):
        u = in_proj(geom, x, mod, l, w_in_p)

        uc_hy = dwconv3(geom, u, OFF_HY, hy_conv_w[l], hy_conv_b[l][None], 0, 3 * W_HY, act=False)
        y_parts = []
        for (tile0, nseq, nc), filt in zip(groups, (filt_p, filt_s)):
            rows = slice(tile0 * TB, (tile0 + nseq * nc) * TB)
            yt = hyena_conv(_to_channel_major(uc_hy[rows], nseq, nc), filt[l], nc, nseq)
            y_parts.append(_to_token_major(yt, nseq, nc))
        y_hy = jnp.concatenate(y_parts, axis=0)

        xs = dwconv3(geom, u, OFF_XBC, ssd_conv_w[l], ssd_conv_b[l][None], 0, D_INNER, act=True)
        bm_t = dwconv3(geom, u, OFF_XBC + D_INNER, ssd_conv_w[l], ssd_conv_b[l][None], D_INNER,
                       G_SSD * N_SSD, act=True, out_dtype=BF16, transpose=True)
        cm = dwconv3(geom, u, OFF_XBC + D_INNER + G_SSD * N_SSD, ssd_conv_w[l], ssd_conv_b[l][None],
                     D_INNER + G_SSD * N_SSD, G_SSD * N_SSD, act=True, out_dtype=BF16)
        dt, la = ssd_dt(geom, u, pad64(ssd_dt_bias[l]), pad64(-jnp.exp(ssd_a_log[l])))
        y_ssd_dir, s_ssd_dir = [], []
        for d in range(2):
            la_g, la_gt = _per_group(la[:, d * H_SSD:(d + 1) * H_SSD], G_SSD, R_SSD)
            dt_g, _ = _per_group(dt[:, d * H_SSD:(d + 1) * H_SSD], G_SSD, R_SSD)
            ys = []
            for gi, (tile0, nseq, nc) in enumerate(groups):
                s0 = None if gi == 0 else _state_to_t(state_ssd[:, l, d], G_SSD, R_SSD)
                y, sf = chunk_scan(xs, 0, cm, 0, bm_t, la_g, la_gt, dt_g, s0, tile0=tile0, nseq=nseq, nc=nc,
                                   g=G_SSD, r=R_SSD, p=P_SSD, n=N_SSD, reverse=(d == 1))
                ys.append(y)
                if gi == 0:
                    s_ssd_dir.append(_state_from_t(sf, R_SSD))
            y_ssd_dir.append(jnp.concatenate(ys, axis=0))
        ssd_states.append(jnp.stack(s_ssd_dir, axis=1))
        y_ssd = ssd_post(geom, y_ssd_dir[0], y_ssd_dir[1], xs, u, d_exp[l][None], ssd_norm_w[l][None])

        q_r, k_t = qk_prep(geom, u, cos_tab, sin_tab)
        y_ret_dir, s_ret_dir = [], []
        for d in range(2):
            la_tok = jnp.broadcast_to(ret_la[l, d][None, :], (geom.t, H_RET))
            la_g, la_gt = _per_group(la_tok, H_RET, 1)
            ys = []
            for gi, (tile0, nseq, nc) in enumerate(groups):
                s0 = None if gi == 0 else _state_to_t(state_ret[:, l, d], H_RET, 1)
                y, sf = chunk_scan(u, OFF_V, q_r, 0, k_t, la_g, la_gt, None, s0, tile0=tile0, nseq=nseq, nc=nc,
                                   g=H_RET, r=1, p=DV_RET, n=DK_RET, reverse=(d == 1))
                ys.append(y)
                if gi == 0:
                    s_ret_dir.append(_state_from_t(sf, 1))
            y_ret_dir.append(jnp.concatenate(ys, axis=0))
        ret_states.append(jnp.stack(s_ret_dir, axis=1))
        y_ret = ret_post(geom, y_ret_dir[0], y_ret_dir[1], u, ret_gn_w[l][None])

        x1, h_ffn, aff_t = merge(geom, l, y_hy, y_ssd, y_ret, u, x, mod, vec3(ln1_g), vec3(ln1_b),
                                 whp, wsp, wrp, wo, rwt, rb)

        ffn_parts = []
        for (tile0, nseq, nc) in groups:
            n = nc * TB
            slot = ec_select(aff_t, tile0, nseq, n)
            ffn_parts.append(moe_experts(l, slot, aff_t, h_ffn, ew1, ew3, ew2, tile0, nseq, n))
        x = resid_ln(geom, l, x1, jnp.concatenate(ffn_parts, axis=0), mod, vec3(ln2_g), vec3(ln2_b))

    y_prompt = x[:geom.tp].reshape(bp, lp, D)
    y_sample = x[geom.tp:].reshape(bs, ls, D)
    return (y_prompt, y_sample, jnp.stack(ssd_states, axis=1), jnp.stack(ret_states, axis=1))
```

```python
import functools
import math

import jax
import jax.numpy as jnp
from jax import lax
from jax.experimental import pallas as pl
from jax.experimental.pallas import tpu as pltpu

F32 = jnp.float32
BF16 = jnp.bfloat16
HIGHEST = lax.Precision.HIGHEST

D = 1024
DEPTH = 4
GRID_W = 64
LN_EPS = 1e-5
W_HY = D
POS_EMB = 33
FILT_HID = 64
HY_TARGET, HY_FAST, HY_SLOW = 1e-2, 0.3, 1.5
D_INNER = 2 * D
P_SSD = 64
H_SSD = D_INNER // P_SSD
G_SSD = 4
N_SSD = 128
R_SSD = H_SSD // G_SSD
CONV_CH = D_INNER + 2 * G_SSD * N_SSD
H_RET = 4
DK_RET = 256
DV_RET = 256
ROPE_BASE = 10000.0
N_EXPERTS = 16
EC_CAPACITY = 2
ALPHA = (2 * DEPTH) ** 0.25

OFF_HY, OFF_Z, OFF_XBC, OFF_Q, OFF_K, OFF_V, OFF_G, OFF_MG, OFF_DT = (
    0, 3072, 5120, 8192, 9216, 10240, 11264, 12288, 15360)
P_PAD = 15872

TB = 256
HALO = 16
V7X_VMEM_LIMIT = 56 * 1024 * 1024


def _cp(sem, vmem=None):
    return pltpu.CompilerParams(dimension_semantics=sem, vmem_limit_bytes=vmem or V7X_VMEM_LIMIT)


def _silu(x):
    return x * jax.nn.sigmoid(x)


def _ln(x):
    mu = jnp.mean(x, axis=-1, keepdims=True)
    xc = x - mu
    var = jnp.mean(xc * xc, axis=-1, keepdims=True)
    return xc * lax.rsqrt(var + LN_EPS)


def _split3(a):
    hi = a.astype(BF16)
    r1 = a - hi.astype(F32)
    mid = r1.astype(BF16)
    lo = (r1 - mid.astype(F32)).astype(BF16)
    return hi, mid, lo


def _dot(a, b):
    return jnp.dot(a, b, preferred_element_type=F32)


def _dot3_l(a_f32, b_bf16):
    hi, mid, lo = _split3(a_f32)
    return _dot(hi, b_bf16) + _dot(mid, b_bf16) + _dot(lo, b_bf16)


def _dot3_r(a_bf16, b_f32):
    hi, mid, lo = _split3(b_f32)
    return _dot(a_bf16, hi) + _dot(a_bf16, mid) + _dot(a_bf16, lo)


class Geom:
    def __init__(self, bp, lp, bs, ls):
        self.bp, self.lp, self.bs, self.ls = bp, lp, bs, ls
        self.tp, self.ts = bp * lp, bs * ls
        self.t = self.tp + self.ts
        self.tpp, self.tps = lp // TB, ls // TB
        self.npt, self.nst = self.tp // TB, self.ts // TB
        self.nt = self.npt + self.nst

    def mod_row(self, i, rows_per_tile=TB):
        npt = self.tp // rows_per_tile
        per = self.ls // rows_per_tile
        return jnp.where(i < npt, 0, 1 + (i - npt) // per)


def _ada_kernel(c_ref, w_ref, b_ref, o_ref):
    s = _silu(c_ref[...])
    o_ref[...] = jnp.dot(s, w_ref[...], precision=HIGHEST, preferred_element_type=F32) + b_ref[...]


def ada_mod(cond16, ada_w, ada_b):
    tn = 1536
    return pl.pallas_call(
        _ada_kernel,
        out_shape=jax.ShapeDtypeStruct((DEPTH, 16, 6 * D), F32),
        grid=(DEPTH, 6 * D // tn),
        in_specs=[pl.BlockSpec((16, D), lambda l, j: (0, 0)),
                  pl.BlockSpec((None, D, tn), lambda l, j: (l, 0, j)),
                  pl.BlockSpec((None, 1, tn), lambda l, j: (l, 0, j))],
        out_specs=pl.BlockSpec((None, 16, tn), lambda l, j: (l, 0, j)),
        compiler_params=_cp(("parallel", "parallel")),
        name="ada_mod",
    )(cond16, ada_w, ada_b.reshape(DEPTH, 1, 6 * D))


def _inproj_kernel(nj, x_ref, sh_ref, sc_ref, w_ref, o_ref, dt_ref, h_ref):
    j = pl.program_id(1)

    @pl.when(j == 0)
    def _():
        h = _ln(x_ref[...]) * (1.0 + sc_ref[...]) + sh_ref[...]
        h_ref[...] = h.astype(BF16)

    acc = _dot(h_ref[...], w_ref[...])
    o_ref[...] = acc.astype(o_ref.dtype)

    @pl.when(j == nj - 1)
    def _():
        dt_ref[...] = acc[:, :128]


def in_proj(geom, x, mod, l, w_in_p):
    tm = next(t for t in (1024, 512, 256) if geom.tp % t == 0 and geom.ls % t == 0)
    tn = 512
    nj = P_PAD // tn
    assert OFF_DT == (nj - 1) * tn
    row = functools.partial(geom.mod_row, rows_per_tile=tm)
    return pl.pallas_call(
        functools.partial(_inproj_kernel, nj),
        out_shape=[jax.ShapeDtypeStruct((geom.t, P_PAD), BF16), jax.ShapeDtypeStruct((geom.t, 128), F32)],
        grid=(geom.t // tm, nj),
        in_specs=[pl.BlockSpec((tm, D), lambda i, j: (i, 0)),
                  pl.BlockSpec((None, None, None, 1, D), lambda i, j: (l, 0, row(i), 0, 0)),
                  pl.BlockSpec((None, None, None, 1, D), lambda i, j: (l, 1, row(i), 0, 0)),
                  pl.BlockSpec((None, D, tn), lambda i, j: (l, 0, j))],
        out_specs=[pl.BlockSpec((tm, tn), lambda i, j: (i, j)),
                   pl.BlockSpec((tm, 128), lambda i, j: (i, 0))],
        scratch_shapes=[pltpu.VMEM((tm, D), BF16)],
        compiler_params=_cp(("parallel", "arbitrary")),
        name="in_proj",
    )(x, mod, mod, w_in_p)


def _dwconv_kernel(geom, act, transpose, x_ref, xp_ref, xn_ref, w_ref, b_ref, o_ref):
    i = pl.program_id(0)
    pos = jnp.where(i < geom.npt, i % geom.tpp, (i - geom.npt) % geom.tps)
    last = jnp.where(i < geom.npt, geom.tpp - 1, geom.tps - 1)
    x = x_ref[...].astype(F32)
    rows = lax.broadcasted_iota(jnp.int32, x.shape, 0)
    prev_row = jnp.where(pos == 0, 0.0, xp_ref[...].astype(F32)[HALO - 1:HALO, :])
    next_row = jnp.where(pos == last, 0.0, xn_ref[...].astype(F32)[0:1, :])
    x_prev = jnp.where(rows == 0, prev_row, pltpu.roll(x, 1, 0))
    x_next = jnp.where(rows == TB - 1, next_row, pltpu.roll(x, TB - 1, 0))
    w = w_ref[...]
    y = x_prev * w[0:1, :] + x * w[1:2, :] + x_next * w[2:3, :] + b_ref[...]
    if act:
        y = _silu(y)
    if transpose:
        y = y.T
    o_ref[...] = y.astype(o_ref.dtype)


def dwconv3(geom, u, col0, w, b, wcol0, ncols, *, act, out_dtype=F32, transpose=False, cw=512):
    nbh = geom.t // HALO
    c0, w0, nj = col0 // cw, wcol0 // cw, ncols // cw
    if transpose:
        out_shape = jax.ShapeDtypeStruct((ncols, geom.t), out_dtype)
        out_spec = pl.BlockSpec((cw, TB), lambda i, j: (j, i))
    else:
        out_shape = jax.ShapeDtypeStruct((geom.t, ncols), out_dtype)
        out_spec = pl.BlockSpec((TB, cw), lambda i, j: (i, j))
    return pl.pallas_call(
        functools.partial(_dwconv_kernel, geom, act, transpose),
        out_shape=out_shape,
        grid=(geom.nt, nj),
        in_specs=[pl.BlockSpec((TB, cw), lambda i, j: (i, c0 + j)),
                  pl.BlockSpec((HALO, cw), lambda i, j: (jnp.maximum(i * (TB // HALO) - 1, 0), c0 + j)),
                  pl.BlockSpec((HALO, cw), lambda i, j: (jnp.minimum((i + 1) * (TB // HALO), nbh - 1), c0 + j)),
                  pl.BlockSpec((3, cw), lambda i, j: (0, w0 + j)),
                  pl.BlockSpec((1, cw), lambda i, j: (0, w0 + j))],
        out_specs=out_spec,
        compiler_params=_cp(("parallel", "parallel")),
        name="dwconv3",
    )(u, u, u, w, b)


def _dt_kernel(u_ref, bias_ref, nega_ref, dt_ref, la_ref):
    dt = jax.nn.softplus(u_ref[...] + bias_ref[...])
    dt_ref[...] = dt
    la_ref[...] = dt * nega_ref[...]


def ssd_dt(geom, u_dt, dt_bias128, neg_a128):
    tm = 512 if geom.t % 512 == 0 else TB
    return pl.pallas_call(
        _dt_kernel,
        out_shape=[jax.ShapeDtypeStruct((geom.t, 128), F32)] * 2,
        grid=(geom.t // tm,),
        in_specs=[pl.BlockSpec((tm, 128), lambda i: (i, 0)),
                  pl.BlockSpec((1, 128), lambda i: (0, 0)),
                  pl.BlockSpec((1, 128), lambda i: (0, 0))],
        out_specs=[pl.BlockSpec((tm, 128), lambda i: (i, 0))] * 2,
        compiler_params=_cp(("parallel",)),
        name="ssd_dt",
    )(u_dt, dt_bias128, neg_a128)


def _qk_kernel(q_ref, k_ref, cos_ref, sin_ref, qo_ref, kto_ref):
    cos, sin = cos_ref[...], sin_ref[...]

    def rope(x):
        halves = [pltpu.roll(x[:, a:a + 128], 64, 1) for a in (0, 128)]
        return x * cos + jnp.concatenate(halves, axis=1) * sin

    qo_ref[...] = rope(q_ref[...].astype(F32)).astype(BF16)
    kto_ref[...] = (rope(k_ref[...].astype(F32)) * (DK_RET ** -0.5)).T.astype(BF16)


def qk_prep(geom, u, cos_tab, sin_tab):
    tab = lambda i, h: (jnp.where(i < geom.npt, geom.tps, (i - geom.npt) % geom.tps), 0)
    return pl.pallas_call(
        _qk_kernel,
        out_shape=[jax.ShapeDtypeStruct((geom.t, H_RET * DK_RET), BF16),
                   jax.ShapeDtypeStruct((H_RET * DK_RET, geom.t), BF16)],
        grid=(geom.nt, H_RET),
        in_specs=[pl.BlockSpec((TB, DK_RET), lambda i, h: (i, OFF_Q // DK_RET + h)),
                  pl.BlockSpec((TB, DK_RET), lambda i, h: (i, OFF_K // DK_RET + h)),
                  pl.BlockSpec((TB, DK_RET), tab),
                  pl.BlockSpec((TB, DK_RET), tab)],
        out_specs=[pl.BlockSpec((TB, DK_RET), lambda i, h: (i, h)),
                   pl.BlockSpec((DK_RET, TB), lambda i, h: (h, i))],
        compiler_params=_cp(("parallel", "parallel")),
        name="qk_prep",
    )(u, u, cos_tab, sin_tab)


def _scan_kernel(r, p, n, nc, reverse, has_dt, has_init, has_alias, *refs):
    it = iter(refs)
    x_ref, c_ref, bt_ref, la_ref, lat_ref = next(it), next(it), next(it), next(it), next(it)
    dt_ref = next(it) if has_dt else None
    s0_ref = next(it) if has_init else None
    if has_alias:
        next(it)
    y_ref, sf_ref, s_ref, yd_ref = next(it), next(it), next(it), next(it)
    c = pl.program_id(2)
    q = TB

    @pl.when(c == 0)
    def _():
        s_ref[...] = s0_ref[...] if has_init else jnp.zeros_like(s_ref)

    ii = lax.broadcasted_iota(jnp.int32, (q, q), 0)
    jj = lax.broadcasted_iota(jnp.int32, (q, q), 1)
    lower = ii >= jj
    tri_a = (jj >= ii) if reverse else lower
    tri_a_bf = tri_a.astype(BF16)
    tri_b_bf = (lower if reverse else (jj >= ii)).astype(BF16)
    end = 0 if reverse else q - 1

    la_col = la_ref[...]
    la_row = lat_ref[...]
    if r == 1:
        la_col = jnp.broadcast_to(la_col, (q, 128))
        la_row = jnp.broadcast_to(la_row, (8, q))
    cs_col = _dot3_r(tri_a_bf, la_col)[:, :r]
    cs_row = _dot3_l(la_row, tri_b_bf)[:r, :]
    tot = cs_col[end:end + 1, :]
    end_decay = jnp.exp(tot - cs_col)
    in_scale = jnp.exp(cs_col)

    if r == 1:
        expand = lambda a: jnp.broadcast_to(a, (q, p))
    else:
        er = lax.broadcasted_iota(jnp.int32, (r, r * p), 0)
        ec = lax.broadcasted_iota(jnp.int32, (r, r * p), 1)
        e_bf = (ec // p == er).astype(BF16)
        expand = lambda a: _dot3_l(a, e_bf)

    x = x_ref[...].astype(F32)
    if has_dt:
        dt = dt_ref[...]
        xd = x * expand(dt)
        xe = x * expand(dt * end_decay)
    else:
        xd = x
        xe = x * expand(end_decay)
    xd_bf, xe_bf = xd.astype(BF16), xe.astype(BF16)
    c_bf = c_ref[...].astype(BF16)
    bt_bf = bt_ref[...].astype(BF16)

    scores = _dot(c_bf, bt_bf)
    for h in range(r):
        dmat = cs_col[:, h:h + 1] - cs_row[h:h + 1, :]
        wgt = scores * jnp.exp(jnp.where(tri_a, dmat, -1e30))
        yd_ref[:, h * p:(h + 1) * p] = _dot(wgt.astype(BF16), xd_bf[:, h * p:(h + 1) * p])

    in_scale_x = expand(in_scale)
    s_old = s_ref[...]
    y_ref[...] = yd_ref[...] + _dot(c_bf, s_old.astype(BF16)) * in_scale_x
    s_new = s_old * in_scale_x[end:end + 1, :] + _dot(bt_bf, xe_bf)
    s_ref[...] = s_new

    @pl.when(c == nc - 1)
    def _():
        sf_ref[...] = s_new


def chunk_scan(x, xcol0, cm, ccol0, bt, la, lat, dt, s0, y_prev, *, tile0, nseq, nc, g, r, p, n, reverse):
    rp = r * p
    xb, cb = xcol0 // rp, ccol0 // n
    ch = (lambda c: nc - 1 - c) if reverse else (lambda c: c)
    tok = lambda b, c: tile0 + b * nc + ch(c)
    in_specs = [pl.BlockSpec((TB, rp), lambda b, gi, c: (tok(b, c), xb + gi)),
                pl.BlockSpec((TB, n), lambda b, gi, c: (tok(b, c), cb + gi)),
                pl.BlockSpec((n, TB), lambda b, gi, c: (gi, tok(b, c))),
                pl.BlockSpec((None, TB, r), lambda b, gi, c: (gi, tok(b, c), 0)),
                pl.BlockSpec((None, r, TB), lambda b, gi, c: (gi, 0, tok(b, c)))]
    args = [x, cm, bt, la, lat]
    if dt is not None:
        in_specs.append(pl.BlockSpec((None, TB, r), lambda b, gi, c: (gi, tok(b, c), 0)))
        args.append(dt)
    if s0 is not None:
        in_specs.append(pl.BlockSpec((None, None, n, rp), lambda b, gi, c: (b, gi, 0, 0)))
        args.append(s0)
    aliases = {}
    if y_prev is not None:
        in_specs.append(pl.BlockSpec(memory_space=pl.ANY))
        args.append(y_prev)
        aliases = {len(args) - 1: 0}
    return pl.pallas_call(
        functools.partial(_scan_kernel, r, p, n, nc, reverse, dt is not None, s0 is not None, y_prev is not None),
        out_shape=[jax.ShapeDtypeStruct((x.shape[0], g * rp), F32),
                   jax.ShapeDtypeStruct((nseq, g, n, rp), F32)],
        grid=(nseq, g, nc),
        in_specs=in_specs,
        out_specs=[pl.BlockSpec((TB, rp), lambda b, gi, c: (tok(b, c), gi)),
                   pl.BlockSpec((None, None, n, rp), lambda b, gi, c: (b, gi, 0, 0))],
        scratch_shapes=[pltpu.VMEM((n, rp), F32), pltpu.VMEM((TB, rp), F32)],
        input_output_aliases=aliases,
        compiler_params=_cp(("parallel", "parallel", "arbitrary")),
        name="chunk_scan",
    )(*args)


def _filt_hidden_kernel(ls, freq_ref, w1a_ref, w1c_ref, w1s_ref, b1_ref, w2t_ref, b2_ref, sf_ref, o_ref):
    pp = lax.broadcasted_iota(jnp.int32, (1, 2 * ls), 1)
    t = jnp.abs(pp - ls).astype(F32)
    t_norm = t / max(ls - 1, 1)
    ang = (2.0 * math.pi) * t * freq_ref[...] / ls
    pre = (w1a_ref[...] * t_norm
           + jnp.dot(w1c_ref[...], jnp.cos(ang), precision=HIGHEST, preferred_element_type=F32)
           - jnp.dot(w1s_ref[...], jnp.sin(ang), precision=HIGHEST, preferred_element_type=F32)
           + b1_ref[...])
    sf = sf_ref[...]
    h1 = jnp.sin(sf[:, 0:1] * pre)
    h2 = jnp.sin(sf[:, 1:2] * (jnp.dot(w2t_ref[...], h1, precision=HIGHEST, preferred_element_type=F32)
                               + b2_ref[...]))
    o_ref[...] = h2


def _filt_out_kernel(ls, hid_ref, wf_ref, wb_ref, delta_ref, bias_ref, o_ref):
    pp = lax.broadcasted_iota(jnp.int32, (1, 2 * ls), 1)
    t_norm = jnp.abs(pp - ls).astype(F32) / max(ls - 1, 1)
    hid = hid_ref[...]
    kf = jnp.dot(wf_ref[...], hid, precision=HIGHEST, preferred_element_type=F32)
    kb = jnp.dot(wb_ref[...], hid, precision=HIGHEST, preferred_element_type=F32)
    window = jnp.exp(-t_norm * jnp.abs(delta_ref[...]))
    k = jnp.where(pp >= ls, kf, kb) * window
    k = jnp.where(pp == 0, 0.0, k)
    o_ref[...] = k + jnp.where(pp == ls, bias_ref[...], 0.0)


def hyena_filter_rows(ls, hy_w1, hy_b1, hy_w2, hy_b2, hy_w3, hy_freq, hy_bias):
    bands = (POS_EMB - 1) // 2
    freqs = jnp.linspace(1e-4, bands - 1, bands, dtype=F32).reshape(bands, 1)
    w1t = jnp.swapaxes(hy_w1, 1, 2)
    hid = pl.pallas_call(
        functools.partial(_filt_hidden_kernel, ls),
        out_shape=jax.ShapeDtypeStruct((DEPTH, FILT_HID, 2 * ls), F32),
        grid=(DEPTH,),
        in_specs=[pl.BlockSpec((bands, 1), lambda l: (0, 0)),
                  pl.BlockSpec((None, FILT_HID, 1), lambda l: (l, 0, 0)),
                  pl.BlockSpec((None, FILT_HID, bands), lambda l: (l, 0, 0)),
                  pl.BlockSpec((None, FILT_HID, bands), lambda l: (l, 0, 0)),
                  pl.BlockSpec((None, FILT_HID, 1), lambda l: (l, 0, 0)),
                  pl.BlockSpec((None, FILT_HID, FILT_HID), lambda l: (l, 0, 0)),
                  pl.BlockSpec((None, FILT_HID, 1), lambda l: (l, 0, 0)),
                  pl.BlockSpec((None, FILT_HID, 2), lambda l: (l, 0, 0))],
        out_specs=pl.BlockSpec((None, FILT_HID, 2 * ls), lambda l: (l, 0, 0)),
        compiler_params=_cp(("parallel",)),
        name="hyena_filter_hidden",
    )(freqs, w1t[:, :, 0:1], w1t[:, :, 1:1 + bands], w1t[:, :, 1 + bands:], hy_b1[..., None],
      jnp.swapaxes(hy_w2, 1, 2), hy_b2[..., None], jnp.swapaxes(hy_freq, 1, 2))
    w3t = jnp.swapaxes(hy_w3, 1, 2).reshape(DEPTH, 2, 2, W_HY, FILT_HID)
    deltas = jnp.linspace(math.log(HY_TARGET) / HY_SLOW, math.log(HY_TARGET) / HY_FAST, W_HY,
                          dtype=F32).reshape(W_HY, 1)
    cb = 256
    return pl.pallas_call(
        functools.partial(_filt_out_kernel, ls),
        out_shape=jax.ShapeDtypeStruct((DEPTH, 2, W_HY, 2 * ls), F32),
        grid=(DEPTH, 2, W_HY // cb),
        in_specs=[pl.BlockSpec((None, FILT_HID, 2 * ls), lambda l, o, j: (l, 0, 0)),
                  pl.BlockSpec((None, None, None, cb, FILT_HID), lambda l, o, j: (l, o, 0, j, 0)),
                  pl.BlockSpec((None, None, None, cb, FILT_HID), lambda l, o, j: (l, o, 1, j, 0)),
                  pl.BlockSpec((cb, 1), lambda l, o, j: (j, 0)),
                  pl.BlockSpec((None, None, cb, 1), lambda l, o, j: (l, o, j, 0))],
        out_specs=pl.BlockSpec((None, None, cb, 2 * ls), lambda l, o, j: (l, o, j, 0)),
        compiler_params=_cp(("parallel", "parallel", "parallel")),
        name="hyena_filter_rows",
    )(hid, w3t, w3t, deltas, hy_bias[..., None])


def _hyena_kernel(nb, b, cbk, v_ref, x1_ref, x2_ref, w_ref, o_ref, acc_ref):
    ls2 = 2 * nb * TB

    def conv(u, o, ch):
        wrow = w_ref[o, pl.ds(ch, 1), :]
        toep = pltpu.roll(jnp.broadcast_to(wrow, (TB, ls2)), 0, 1, stride=1, stride_axis=0).astype(BF16)
        acc_ref[...] = jnp.zeros_like(acc_ref)
        for mm in range(2 * nb - 1):
            m = mm - (nb - 1)
            lo, hi = max(0, -m), min(nb, nb - m)
            part = _dot(u[lo * b:hi * b, :].astype(BF16), toep[:, TB * (mm + 1):TB * (mm + 2)])
            acc_ref[(lo + m) * b:(hi + m) * b, :] += part
        return acc_ref[...]

    def body(ch, carry):
        z = x1_ref[ch] * conv(v_ref[ch], 0, ch)
        o_ref[ch] = x2_ref[ch] * conv(z, 1, ch)
        return carry

    lax.fori_loop(0, cbk, body, 0)


def hyena_conv(uct, filt_l, nb, b):
    cbk = 8
    rows = nb * b
    return pl.pallas_call(
        functools.partial(_hyena_kernel, nb, b, cbk),
        out_shape=jax.ShapeDtypeStruct((W_HY, rows, TB), F32),
        grid=(W_HY // cbk,),
        in_specs=[pl.BlockSpec((None, cbk, rows, TB), lambda j: (0, j, 0, 0)),
                  pl.BlockSpec((None, cbk, rows, TB), lambda j: (1, j, 0, 0)),
                  pl.BlockSpec((None, cbk, rows, TB), lambda j: (2, j, 0, 0)),
                  pl.BlockSpec((2, cbk, 2 * nb * TB), lambda j: (0, j, 0))],
        out_specs=pl.BlockSpec((cbk, rows, TB), lambda j: (j, 0, 0)),
        scratch_shapes=[pltpu.VMEM((rows, TB), F32)],
        compiler_params=_cp(("parallel",)),
        name="hyena_conv",
    )(uct, uct, uct, filt_l)


def _ssd_post_kernel(yf_ref, yb_ref, xs_ref, z_ref, d_ref, nw_ref, o_ref):
    y = yf_ref[...] + yb_ref[...] + xs_ref[...] * d_ref[...]
    y = y * _silu(z_ref[...].astype(F32))
    y = y * lax.rsqrt(jnp.mean(y * y, axis=-1, keepdims=True) + LN_EPS)
    o_ref[...] = (y * nw_ref[...]).astype(o_ref.dtype)


def ssd_post(geom, yf, yb, xs, u, d_exp, norm_w):
    gw = D_INNER // G_SSD
    tm = 512 if geom.t % 512 == 0 else TB
    blk = pl.BlockSpec((tm, gw), lambda i, gi: (i, gi))
    vec = pl.BlockSpec((1, gw), lambda i, gi: (0, gi))
    return pl.pallas_call(
        _ssd_post_kernel,
        out_shape=jax.ShapeDtypeStruct((geom.t, D_INNER), BF16),
        grid=(geom.t // tm, G_SSD),
        in_specs=[blk, blk, blk, pl.BlockSpec((tm, gw), lambda i, gi: (i, OFF_Z // gw + gi)), vec, vec],
        out_specs=blk,
        compiler_params=_cp(("parallel", "parallel")),
        name="ssd_post",
    )(yf, yb, xs, u, d_exp, norm_w)


def _ret_post_kernel(of_ref, ob_ref, g_ref, w_ref, o_ref):
    o = _ln(of_ref[...] + ob_ref[...]) * w_ref[...]
    o_ref[...] = (o * _silu(g_ref[...].astype(F32))).astype(o_ref.dtype)


def ret_post(geom, of, ob, u, gn_w):
    tm = 512 if geom.t % 512 == 0 else TB
    blk = pl.BlockSpec((tm, DV_RET), lambda i, h: (i, h))
    return pl.pallas_call(
        _ret_post_kernel,
        out_shape=jax.ShapeDtypeStruct((geom.t, H_RET * DV_RET), BF16),
        grid=(geom.t // tm, H_RET),
        in_specs=[blk, blk, pl.BlockSpec((tm, DV_RET), lambda i, h: (i, OFF_G // DV_RET + h)),
                  pl.BlockSpec((1, DV_RET), lambda i, h: (0, h))],
        out_specs=blk,
        compiler_params=_cp(("parallel", "parallel")),
        name="ret_post",
    )(of, ob, u, gn_w)


def _merge_kernel(yhy_ref, yssd_ref, yret_ref, mg0_ref, mg1_ref, mg2_ref, x_ref, g1_ref, sh2_ref, sc2_ref,
                  lg_ref, lb_ref, whp_ref, wsp_ref, wrp_ref, wo_ref, rwt_ref, rb_ref,
                  x1_ref, h_ref, aff_ref):
    gate = lambda r: jax.nn.sigmoid(r[...].astype(F32))
    merged = (gate(mg0_ref) * _dot(yhy_ref[...].astype(BF16), whp_ref[...])
              + gate(mg1_ref) * _dot(yssd_ref[...], wsp_ref[...])
              + gate(mg2_ref) * _dot(yret_ref[...], wrp_ref[...]))
    mix = _dot(merged.astype(BF16), wo_ref[...])
    x1 = _ln(ALPHA * x_ref[...] + g1_ref[...] * mix) * lg_ref[...] + lb_ref[...]
    x1_ref[...] = x1
    h = _ln(x1) * (1.0 + sc2_ref[...]) + sh2_ref[...]
    h_ref[...] = h.astype(BF16)
    logits = lax.dot_general(rwt_ref[...], h, (((1,), (1,)), ((), ())), precision=HIGHEST,
                             preferred_element_type=F32) + rb_ref[...]
    e = jnp.exp(logits - jnp.max(logits, axis=0, keepdims=True))
    aff_ref[...] = e / jnp.sum(e, axis=0, keepdims=True)


def merge(geom, l, y_hy, y_ssd, y_ret, u, x, mod, ln_g, ln_b, whp, wsp, wrp, wo, rwt, rb):
    tm = TB
    row = geom.mod_row
    tok = lambda w: pl.BlockSpec((tm, w), lambda i: (i, 0))
    mgs = [pl.BlockSpec((tm, D), lambda i, k=k: (i, OFF_MG // D + k)) for k in range(3)]
    modspec = lambda k: pl.BlockSpec((None, None, None, 1, D), lambda i: (l, k, row(i), 0, 0))
    vec = pl.BlockSpec((None, 1, D), lambda i: (l, 0, 0))
    wspec = lambda kk: pl.BlockSpec((None, kk, D), lambda i: (l, 0, 0))
    return pl.pallas_call(
        _merge_kernel,
        out_shape=[jax.ShapeDtypeStruct((geom.t, D), F32), jax.ShapeDtypeStruct((geom.t, D), BF16),
                   jax.ShapeDtypeStruct((N_EXPERTS, geom.t), F32)],
        grid=(geom.t // tm,),
        in_specs=[tok(D), tok(D_INNER), tok(D)] + mgs + [tok(D), modspec(2), modspec(3), modspec(4), vec, vec,
                  wspec(D), wspec(D_INNER), wspec(D), wspec(D),
                  pl.BlockSpec((None, N_EXPERTS, D), lambda i: (l, 0, 0)),
                  pl.BlockSpec((None, N_EXPERTS, 1), lambda i: (l, 0, 0))],
        out_specs=[tok(D), tok(D), pl.BlockSpec((N_EXPERTS, tm), lambda i: (0, i))],
        compiler_params=_cp(("parallel",)),
        name="merge",
    )(y_hy, y_ssd, y_ret, u, u, u, x, mod, mod, mod, ln_g, ln_b, whp, wsp, wrp, wo, rwt, rb)


def _select_kernel(n, cap, aff_ref, slot_ref, bnd_ref):
    aff = aff_ref[...]
    bits = pltpu.bitcast(aff, jnp.int32)

    def bit_step(k, thr):
        cand = thr | jnp.left_shift(jnp.int32(1), 30 - k)
        cnt = jnp.sum(jnp.where(bits >= cand, 1.0, 0.0), axis=1, keepdims=True)
        return jnp.where(cnt >= cap, cand, thr)

    thr = lax.fori_loop(0, 31, bit_step, jnp.zeros((N_EXPERTS, 1), jnp.int32))
    gt = bits > thr
    eq = bits == thr
    need = cap - jnp.sum(jnp.where(gt, 1.0, 0.0), axis=1, keepdims=True)

    kk = lax.broadcasted_iota(jnp.int32, (TB, TB), 0)
    jj = lax.broadcasted_iota(jnp.int32, (TB, TB), 1)
    before = (kk < jj).astype(BF16)

    def excl_cumsum(mask):
        ones = jnp.where(mask, 1.0, 0.0)
        outs, carry = [], jnp.zeros((N_EXPERTS, 1), F32)
        for j in range(n // TB):
            blk = ones[:, j * TB:(j + 1) * TB]
            outs.append(_dot(blk.astype(BF16), before) + carry)
            carry = carry + jnp.sum(blk, axis=1, keepdims=True)
        return jnp.concatenate(outs, axis=1)

    sel = gt | (eq & (excl_cumsum(eq) < need))
    slot_ref[...] = jnp.where(sel, excl_cumsum(sel), -1.0)
    tt = lax.broadcasted_iota(jnp.int32, (n, 128), 0)
    tj = lax.broadcasted_iota(jnp.int32, (n, 128), 1)
    bnd_ref[...] = _dot(jnp.where(sel, 1.0, 0.0).astype(BF16), (tt < tj * TB).astype(BF16))


def ec_select(aff_t, tile0, nseq, n):
    cap = EC_CAPACITY * n // N_EXPERTS
    nt = n // TB
    assert tile0 % nt == 0 and nt < 128
    return pl.pallas_call(
        functools.partial(_select_kernel, n, cap),
        out_shape=[jax.ShapeDtypeStruct((N_EXPERTS, nseq * n), F32),
                   jax.ShapeDtypeStruct((nseq, N_EXPERTS, 128), F32)],
        grid=(nseq,),
        in_specs=[pl.BlockSpec((N_EXPERTS, n), lambda s: (0, tile0 // nt + s))],
        out_specs=[pl.BlockSpec((N_EXPERTS, n), lambda s: (0, s)),
                   pl.BlockSpec((None, N_EXPERTS, 128), lambda s: (s, 0, 0))],
        compiler_params=_cp(("parallel",)),
        name="ec_select",
    )(aff_t)


def _moe_kernel(nd, ntd, cap, sc, b_ref, slot_ref, aff_ref, h_ref, w1_ref, w3_ref, w2_ref, o_ref,
                xe_ref, g_ref, yh_ref, yl_ref):
    blk, e = pl.program_id(0), pl.program_id(1)

    @pl.when(e == 0)
    def _():
        o_ref[...] = jnp.zeros_like(o_ref)

    xe_ref[...] = jnp.zeros_like(xe_ref)
    g_ref[...] = jnp.zeros_like(g_ref)

    def for_pairs(fn):
        def tile_body(t, carry):
            d = 0 if nd == 1 else (t if ntd == 1 else t // ntd)
            slot = slot_ref[e, pl.ds(t, 1), :]
            tok0 = pl.multiple_of(t * TB, TB)
            if ntd > 1:
                base = ((blk * nd + d) * N_EXPERTS + e) * BOUNDS_STRIDE + (t - d * ntd)
                lo, hi = b_ref[base], b_ref[base + 1]
            for c in range(cap // sc):
                row0 = d * cap + c * sc
                if nd > 1:
                    row0 = pl.multiple_of(row0, sc)

                def run(c=c, row0=row0):
                    jcol = (lax.broadcasted_iota(jnp.int32, (sc, 1), 0) + c * sc).astype(F32)
                    fn(t, tok0, row0, slot == jcol)

                if ntd > 1:
                    pl.when((hi > c * sc) & (lo < (c + 1) * sc))(run)
                else:
                    run()
            return carry

        lax.fori_loop(0, nd * ntd, tile_body, 0)

    def gather(t, tok0, row0, pick):
        pick_bf = jnp.where(pick, 1.0, 0.0).astype(BF16)
        gate = aff_ref[e, pl.ds(t, 1), :]
        xe_ref[pl.ds(row0, sc), :] += _dot(pick_bf, h_ref[pl.ds(tok0, TB), :])
        g_ref[pl.ds(row0, sc), :] += jnp.sum(jnp.where(pick, gate, 0.0), axis=1, keepdims=True)

    for_pairs(gather)

    xe = xe_ref[...].astype(BF16)
    hid = _silu(_dot(xe, w1_ref[...])) * _dot(xe, w3_ref[...]) * g_ref[...]
    ye = _dot(hid.astype(BF16), w2_ref[...])
    ye_hi = ye.astype(BF16)
    yh_ref[...] = ye_hi
    yl_ref[...] = (ye - ye_hi.astype(F32)).astype(BF16)

    def scatter(t, tok0, row0, pick):
        pick_bf = jnp.where(pick, 1.0, 0.0).astype(BF16)
        dn = (((0,), (0,)), ((), ()))
        o_ref[pl.ds(tok0, TB), :] += (
            lax.dot_general(pick_bf, yh_ref[pl.ds(row0, sc), :], dn, preferred_element_type=F32)
            + lax.dot_general(pick_bf, yl_ref[pl.ds(row0, sc), :], dn, preferred_element_type=F32))

    for_pairs(scatter)


MOE_BLOCK_TOKENS = 4096
BOUNDS_STRIDE = 32


def moe_experts(l, slot, bounds, aff_t, h, w1, w3, w2, tile0, nseq, n):
    cap = EC_CAPACITY * n // N_EXPERTS
    ntd = n // TB
    nd = max(d for d in range(1, nseq + 1) if nseq % d == 0 and d * n <= MOE_BLOCK_TOKENS)
    tpb = nd * ntd
    assert tile0 % tpb == 0 and tpb % 8 == 0
    sc = min(cap, 128)
    nslot = nd * cap
    slot3 = slot.reshape(N_EXPERTS, nseq * ntd, TB)
    aff3 = aff_t.reshape(N_EXPERTS, aff_t.shape[1] // TB, TB)
    assert ntd + 1 <= BOUNDS_STRIDE
    bflat = (bounds[:, :, :BOUNDS_STRIDE].astype(jnp.int32).reshape(-1) if ntd > 1
             else jnp.zeros((8,), jnp.int32))
    wspec = pl.BlockSpec((None, None, D, D), lambda s, e, b: (l, e, 0, 0))
    grid_spec = pltpu.PrefetchScalarGridSpec(
        num_scalar_prefetch=1,
        grid=(nseq // nd, N_EXPERTS),
        in_specs=[pl.BlockSpec((N_EXPERTS, tpb, TB), lambda s, e, b: (0, s, 0)),
                  pl.BlockSpec((N_EXPERTS, tpb, TB), lambda s, e, b: (0, tile0 // tpb + s, 0)),
                  pl.BlockSpec((tpb * TB, D), lambda s, e, b: (tile0 // tpb + s, 0), pipeline_mode=pl.Buffered(1)),
                  wspec, wspec, wspec],
        out_specs=pl.BlockSpec((tpb * TB, D), lambda s, e, b: (s, 0), pipeline_mode=pl.Buffered(1)),
        scratch_shapes=[pltpu.VMEM((nslot, D), F32), pltpu.VMEM((nslot, 1), F32),
                        pltpu.VMEM((nslot, D), BF16), pltpu.VMEM((nslot, D), BF16)])
    return pl.pallas_call(
        functools.partial(_moe_kernel, nd, ntd, cap, sc),
        out_shape=jax.ShapeDtypeStruct((nseq * n, D), F32),
        grid_spec=grid_spec,
        compiler_params=_cp(("parallel", "arbitrary")),
        name="moe_experts",
    )(bflat, slot3, aff3, h, w1, w3, w2)


def _resid_kernel(x_ref, f_ref, g_ref, lg_ref, lb_ref, o_ref):
    o_ref[...] = _ln(ALPHA * x_ref[...] + g_ref[...] * f_ref[...]) * lg_ref[...] + lb_ref[...]


def resid_ln(geom, l, x1, ffn, mod, ln_g, ln_b):
    tm = 512 if (geom.tp % 512 == 0 and geom.ls % 512 == 0) else TB
    row = functools.partial(geom.mod_row, rows_per_tile=tm)
    tok = pl.BlockSpec((tm, D), lambda i: (i, 0))
    vec = pl.BlockSpec((None, 1, D), lambda i: (l, 0, 0))
    return pl.pallas_call(
        _resid_kernel,
        out_shape=jax.ShapeDtypeStruct((geom.t, D), F32),
        grid=(geom.t // tm,),
        in_specs=[tok, tok, pl.BlockSpec((None, None, None, 1, D), lambda i: (l, 5, row(i), 0, 0)), vec, vec],
        out_specs=tok,
        compiler_params=_cp(("parallel",)),
        name="resid_ln",
    )(x1, ffn, mod, ln_g, ln_b)


def _rope_tables(ls):
    quarter = DK_RET // 4
    inv_freq = ROPE_BASE ** (-jnp.arange(quarter, dtype=F32) / quarter)
    pos = jnp.arange(ls, dtype=jnp.int32)
    row_ang = (pos // GRID_W).astype(F32)[:, None] * inv_freq[None]
    col_ang = (pos % GRID_W).astype(F32)[:, None] * inv_freq[None]
    cos = jnp.concatenate([jnp.cos(row_ang)] * 2 + [jnp.cos(col_ang)] * 2, axis=1)
    sin = jnp.concatenate([-jnp.sin(row_ang), jnp.sin(row_ang), -jnp.sin(col_ang), jnp.sin(col_ang)], axis=1)
    cos = jnp.concatenate([cos, jnp.ones((TB, DK_RET), F32)], axis=0)
    sin = jnp.concatenate([sin, jnp.zeros((TB, DK_RET), F32)], axis=0)
    return cos, sin


def _to_channel_major(uc, b, nb):
    x = uc.reshape(b, nb, TB, 3, W_HY)
    return jnp.transpose(x, (3, 4, 1, 0, 2)).reshape(3, W_HY, nb * b, TB)


def _to_token_major(yt, b, nb):
    x = yt.reshape(W_HY, nb, b, TB)
    return jnp.transpose(x, (2, 1, 3, 0)).reshape(b * nb * TB, W_HY)


def _state_to_t(s, g, r):
    b, h, p, n = s.shape
    return jnp.transpose(s.reshape(b, g, r, p, n), (0, 1, 4, 2, 3)).reshape(b, g, n, r * p)


def _state_from_t(st, r):
    b, g, n, rp = st.shape
    p = rp // r
    return jnp.transpose(st.reshape(b, g, n, r, p), (0, 1, 3, 4, 2)).reshape(b, g * r, p, n)


def _per_group(a, g, r):
    t = a.shape[0]
    a3 = a.reshape(t, g, r)
    return jnp.transpose(a3, (1, 0, 2)), jnp.transpose(a3, (1, 2, 0))


def kernel(x_prompt, x_sample, c, state_ssd, state_ret, c_ctx, w_in, ada_w, ada_b, hy_conv_w, hy_conv_b, hy_w1, hy_b1, hy_w2, hy_b2, hy_w3, hy_freq, hy_bias, hy_proj, ssd_conv_w, ssd_conv_b, ssd_dt_bias, ssd_a_log, ssd_d, ssd_norm_w, ssd_proj, ret_decay, ret_gn_w, ret_proj, w_out, ln1_g, ln1_b, ln2_g, ln2_b, router_w, router_b, exp_w1, exp_w3, exp_w2):
    bp, lp, _ = x_prompt.shape
    bs, ls, _ = x_sample.shape
    geom = Geom(bp, lp, bs, ls)
    assert lp % TB == 0 and ls % TB == 0 and bs + 1 <= 16 and bp % 8 == 0 and bs % 8 == 0 and (bp * lp) % ls == 0

    w_in_p = jnp.concatenate([w_in[:, :, :OFF_Q], w_in[:, :, OFF_Q + 64:], w_in[:, :, OFF_Q:OFF_Q + 64],
                              jnp.zeros((DEPTH, D, P_PAD - OFF_DT - 64), F32)], axis=2).astype(BF16)
    whp, wsp, wrp, wo = (w.astype(BF16) for w in (hy_proj, ssd_proj, ret_proj, w_out))
    ew1, ew3, ew2 = (w.astype(BF16) for w in (exp_w1, exp_w3, exp_w2))
    rwt = jnp.swapaxes(router_w, 1, 2)
    rb = router_b[..., None]
    vec3 = lambda a: a[:, None, :]
    cond16 = jnp.zeros((16, D), F32).at[0].set(c_ctx).at[1:1 + bs].set(c)
    mod = ada_mod(cond16, ada_w, ada_b)
    mod = jnp.transpose(mod.reshape(DEPTH, 16, 6, D), (0, 2, 1, 3))[:, :, :, None, :]

    filt_p = hyena_filter_rows(lp, hy_w1, hy_b1, hy_w2, hy_b2, hy_w3, hy_freq, hy_bias)
    filt_s = hyena_filter_rows(ls, hy_w1, hy_b1, hy_w2, hy_b2, hy_w3, hy_freq, hy_bias)
    cos_tab, sin_tab = _rope_tables(ls)
    pad64 = lambda a: jnp.pad(a.reshape(1, 2 * H_SSD), ((0, 0), (0, 128 - 2 * H_SSD)))
    d_exp = jnp.repeat(ssd_d, P_SSD, axis=1)
    ret_la = -jnp.exp(ret_decay)

    x = jnp.concatenate([x_prompt.reshape(geom.tp, D), x_sample.reshape(geom.ts, D)], axis=0)
    ssd_states, ret_states = [], []
    groups = ((0, bp, geom.tpp), (geom.npt, bs, geom.tps))

    for l in range(DEPTH):
        u, u_dt = in_proj(geom, x, mod, l, w_in_p)

        uc_hy = dwconv3(geom, u, OFF_HY, hy_conv_w[l], hy_conv_b[l][None], 0, 3 * W_HY, act=False)
        y_parts = []
        for (tile0, nseq, nc), filt in zip(groups, (filt_p, filt_s)):
            rows = slice(tile0 * TB, (tile0 + nseq * nc) * TB)
            yt = hyena_conv(_to_channel_major(uc_hy[rows], nseq, nc), filt[l], nc, nseq)
            y_parts.append(_to_token_major(yt, nseq, nc))
        y_hy = jnp.concatenate(y_parts, axis=0)

        xs = dwconv3(geom, u, OFF_XBC, ssd_conv_w[l], ssd_conv_b[l][None], 0, D_INNER, act=True)
        bm_t = dwconv3(geom, u, OFF_XBC + D_INNER, ssd_conv_w[l], ssd_conv_b[l][None], D_INNER,
                       G_SSD * N_SSD, act=True, out_dtype=BF16, transpose=True)
        cm = dwconv3(geom, u, OFF_XBC + D_INNER + G_SSD * N_SSD, ssd_conv_w[l], ssd_conv_b[l][None],
                     D_INNER + G_SSD * N_SSD, G_SSD * N_SSD, act=True, out_dtype=BF16)
        dt, la = ssd_dt(geom, u_dt, pad64(ssd_dt_bias[l]), pad64(-jnp.exp(ssd_a_log[l])))
        y_ssd_dir, s_ssd_dir = [], []
        for d in range(2):
            la_g, la_gt = _per_group(la[:, d * H_SSD:(d + 1) * H_SSD], G_SSD, R_SSD)
            dt_g, _ = _per_group(dt[:, d * H_SSD:(d + 1) * H_SSD], G_SSD, R_SSD)
            y = None
            for gi, (tile0, nseq, nc) in enumerate(groups):
                s0 = None if gi == 0 else _state_to_t(state_ssd[:, l, d], G_SSD, R_SSD)
                y, sf = chunk_scan(xs, 0, cm, 0, bm_t, la_g, la_gt, dt_g, s0, y, tile0=tile0, nseq=nseq, nc=nc,
                                   g=G_SSD, r=R_SSD, p=P_SSD, n=N_SSD, reverse=(d == 1))
                if gi == 0:
                    s_ssd_dir.append(_state_from_t(sf, R_SSD))
            y_ssd_dir.append(y)
        ssd_states.append(jnp.stack(s_ssd_dir, axis=1))
        y_ssd = ssd_post(geom, y_ssd_dir[0], y_ssd_dir[1], xs, u, d_exp[l][None], ssd_norm_w[l][None])

        q_r, k_t = qk_prep(geom, u, cos_tab, sin_tab)
        y_ret_dir, s_ret_dir = [], []
        for d in range(2):
            la_tok = jnp.broadcast_to(ret_la[l, d][None, :], (geom.t, H_RET))
            la_g, la_gt = _per_group(la_tok, H_RET, 1)
            y = None
            for gi, (tile0, nseq, nc) in enumerate(groups):
                s0 = None if gi == 0 else _state_to_t(state_ret[:, l, d], H_RET, 1)
                y, sf = chunk_scan(u, OFF_V, q_r, 0, k_t, la_g, la_gt, None, s0, y, tile0=tile0, nseq=nseq, nc=nc,
                                   g=H_RET, r=1, p=DV_RET, n=DK_RET, reverse=(d == 1))
                if gi == 0:
                    s_ret_dir.append(_state_from_t(sf, 1))
            y_ret_dir.append(y)
        ret_states.append(jnp.stack(s_ret_dir, axis=1))
        y_ret = ret_post(geom, y_ret_dir[0], y_ret_dir[1], u, ret_gn_w[l][None])

        x1, h_ffn, aff_t = merge(geom, l, y_hy, y_ssd, y_ret, u, x, mod, vec3(ln1_g), vec3(ln1_b),
                                 whp, wsp, wrp, wo, rwt, rb)

        ffn_parts = []
        for (tile0, nseq, nc) in groups:
            n = nc * TB
            slot, bounds = ec_select(aff_t, tile0, nseq, n)
            ffn_parts.append(moe_experts(l, slot, bounds, aff_t, h_ffn, ew1, ew3, ew2, tile0, nseq, n))
        x = resid_ln(geom, l, x1, jnp.concatenate(ffn_parts, axis=0), mod, vec3(ln2_g), vec3(ln2_b))

    y_prompt = x[:geom.tp].reshape(bp, lp, D)
    y_sample = x[geom.tp:].reshape(bs, ls, D)
    return (y_prompt, y_sample, jnp.stack(ssd_states, axis=1), jnp.stack(ret_states, axis=1))
```

```python
import functools
import math

import jax
import jax.numpy as jnp
from jax import lax
from jax.experimental import pallas as pl
from jax.experimental.pallas import tpu as pltpu

F32 = jnp.float32
BF16 = jnp.bfloat16
HIGHEST = lax.Precision.HIGHEST

D = 1024
DEPTH = 4
GRID_W = 64
LN_EPS = 1e-5
W_HY = D
POS_EMB = 33
FILT_HID = 64
HY_TARGET, HY_FAST, HY_SLOW = 1e-2, 0.3, 1.5
D_INNER = 2 * D
P_SSD = 64
H_SSD = D_INNER // P_SSD
G_SSD = 4
N_SSD = 128
R_SSD = H_SSD // G_SSD
CONV_CH = D_INNER + 2 * G_SSD * N_SSD
H_RET = 4
DK_RET = 256
DV_RET = 256
ROPE_BASE = 10000.0
N_EXPERTS = 16
EC_CAPACITY = 2
ALPHA = (2 * DEPTH) ** 0.25

OFF_HY, OFF_Z, OFF_XBC, OFF_Q, OFF_K, OFF_V, OFF_G, OFF_MG, OFF_DT = (
    0, 3072, 5120, 8192, 9216, 10240, 11264, 12288, 15360)
P_PAD = 15872

TB = 256
HALO = 16
V7X_VMEM_LIMIT = 56 * 1024 * 1024


def _cp(sem, vmem=None):
    return pltpu.CompilerParams(dimension_semantics=sem, vmem_limit_bytes=vmem or V7X_VMEM_LIMIT)


def _silu(x):
    return x * jax.nn.sigmoid(x)


def _ln(x):
    mu = jnp.mean(x, axis=-1, keepdims=True)
    xc = x - mu
    var = jnp.mean(xc * xc, axis=-1, keepdims=True)
    return xc * lax.rsqrt(var + LN_EPS)


def _split2(a):
    hi = a.astype(BF16)
    return hi, (a - hi.astype(F32)).astype(BF16)


def _dot(a, b):
    return jnp.dot(a, b, preferred_element_type=F32)


def _dot2_l(a_f32, b_bf16):
    hi, lo = _split2(a_f32)
    return _dot(hi, b_bf16) + _dot(lo, b_bf16)


def _dot2_r(a_bf16, b_f32):
    hi, lo = _split2(b_f32)
    return _dot(a_bf16, hi) + _dot(a_bf16, lo)


class Geom:
    def __init__(self, bp, lp, bs, ls):
        self.bp, self.lp, self.bs, self.ls = bp, lp, bs, ls
        self.tp, self.ts = bp * lp, bs * ls
        self.t = self.tp + self.ts
        self.tpp, self.tps = lp // TB, ls // TB
        self.npt, self.nst = self.tp // TB, self.ts // TB
        self.nt = self.npt + self.nst

    def mod_row(self, i, rows_per_tile=TB):
        npt = self.tp // rows_per_tile
        per = self.ls // rows_per_tile
        return jnp.where(i < npt, 0, 1 + (i - npt) // per)


def _ada_kernel(c_ref, w_ref, b_ref, o_ref):
    s = _silu(c_ref[...])
    o_ref[...] = jnp.dot(s, w_ref[...], precision=HIGHEST, preferred_element_type=F32) + b_ref[...]


def ada_mod(cond16, ada_w, ada_b):
    tn = 1536
    return pl.pallas_call(
        _ada_kernel,
        out_shape=jax.ShapeDtypeStruct((DEPTH, 16, 6 * D), F32),
        grid=(DEPTH, 6 * D // tn),
        in_specs=[pl.BlockSpec((16, D), lambda l, j: (0, 0)),
                  pl.BlockSpec((None, D, tn), lambda l, j: (l, 0, j)),
                  pl.BlockSpec((None, 1, tn), lambda l, j: (l, 0, j))],
        out_specs=pl.BlockSpec((None, 16, tn), lambda l, j: (l, 0, j)),
        compiler_params=_cp(("parallel", "parallel")),
        name="ada_mod",
    )(cond16, ada_w, ada_b.reshape(DEPTH, 1, 6 * D))


def _inproj_kernel(nj, x_ref, sh_ref, sc_ref, w_ref, o_ref, dt_ref, h_ref):
    j = pl.program_id(1)

    @pl.when(j == 0)
    def _():
        h = _ln(x_ref[...]) * (1.0 + sc_ref[...]) + sh_ref[...]
        h_ref[...] = h.astype(BF16)

    acc = _dot(h_ref[...], w_ref[...])
    o_ref[...] = acc.astype(o_ref.dtype)

    @pl.when(j == nj - 1)
    def _():
        dt_ref[...] = acc[:, :128]


def in_proj(geom, x, mod, l, w_in_p):
    tm = next(t for t in (2048, 1024, 512, 256) if geom.tp % t == 0 and geom.ls % t == 0)
    tn = 512
    nj = P_PAD // tn
    assert OFF_DT == (nj - 1) * tn
    row = functools.partial(geom.mod_row, rows_per_tile=tm)
    return pl.pallas_call(
        functools.partial(_inproj_kernel, nj),
        out_shape=[jax.ShapeDtypeStruct((geom.t, P_PAD), BF16), jax.ShapeDtypeStruct((geom.t, 128), F32)],
        grid=(geom.t // tm, nj),
        in_specs=[pl.BlockSpec((tm, D), lambda i, j: (i, 0)),
                  pl.BlockSpec((None, None, None, 1, D), lambda i, j: (l, 0, row(i), 0, 0)),
                  pl.BlockSpec((None, None, None, 1, D), lambda i, j: (l, 1, row(i), 0, 0)),
                  pl.BlockSpec((None, D, tn), lambda i, j: (l, 0, j))],
        out_specs=[pl.BlockSpec((tm, tn), lambda i, j: (i, j)),
                   pl.BlockSpec((tm, 128), lambda i, j: (i, 0))],
        scratch_shapes=[pltpu.VMEM((tm, D), BF16)],
        compiler_params=_cp(("parallel", "arbitrary")),
        name="in_proj",
    )(x, mod, mod, w_in_p)


def _dwconv_kernel(geom, act, transpose, x_ref, xp_ref, xn_ref, w_ref, b_ref, o_ref):
    i = pl.program_id(0)
    pos = jnp.where(i < geom.npt, i % geom.tpp, (i - geom.npt) % geom.tps)
    last = jnp.where(i < geom.npt, geom.tpp - 1, geom.tps - 1)
    x = x_ref[...].astype(F32)
    rows = lax.broadcasted_iota(jnp.int32, x.shape, 0)
    prev_row = jnp.where(pos == 0, 0.0, xp_ref[...].astype(F32)[HALO - 1:HALO, :])
    next_row = jnp.where(pos == last, 0.0, xn_ref[...].astype(F32)[0:1, :])
    x_prev = jnp.where(rows == 0, prev_row, pltpu.roll(x, 1, 0))
    x_next = jnp.where(rows == TB - 1, next_row, pltpu.roll(x, TB - 1, 0))
    w = w_ref[...]
    y = x_prev * w[0:1, :] + x * w[1:2, :] + x_next * w[2:3, :] + b_ref[...]
    if act:
        y = _silu(y)
    if transpose:
        y = y.T
    o_ref[...] = y.astype(o_ref.dtype)


def dwconv3(geom, u, col0, w, b, wcol0, ncols, *, act, out_dtype=F32, transpose=False, cw=512):
    nbh = geom.t // HALO
    c0, w0, nj = col0 // cw, wcol0 // cw, ncols // cw
    if transpose:
        out_shape = jax.ShapeDtypeStruct((ncols, geom.t), out_dtype)
        out_spec = pl.BlockSpec((cw, TB), lambda i, j: (j, i))
    else:
        out_shape = jax.ShapeDtypeStruct((geom.t, ncols), out_dtype)
        out_spec = pl.BlockSpec((TB, cw), lambda i, j: (i, j))
    return pl.pallas_call(
        functools.partial(_dwconv_kernel, geom, act, transpose),
        out_shape=out_shape,
        grid=(geom.nt, nj),
        in_specs=[pl.BlockSpec((TB, cw), lambda i, j: (i, c0 + j)),
                  pl.BlockSpec((HALO, cw), lambda i, j: (jnp.maximum(i * (TB // HALO) - 1, 0), c0 + j)),
                  pl.BlockSpec((HALO, cw), lambda i, j: (jnp.minimum((i + 1) * (TB // HALO), nbh - 1), c0 + j)),
                  pl.BlockSpec((3, cw), lambda i, j: (0, w0 + j)),
                  pl.BlockSpec((1, cw), lambda i, j: (0, w0 + j))],
        out_specs=out_spec,
        compiler_params=_cp(("parallel", "parallel")),
        name="dwconv3",
    )(u, u, u, w, b)


def _dt_kernel(u_ref, bias_ref, nega_ref, dt_ref, la_ref):
    dt = jax.nn.softplus(u_ref[...] + bias_ref[...])
    dt_ref[...] = dt
    la_ref[...] = dt * nega_ref[...]


def ssd_dt(geom, u_dt, dt_bias128, neg_a128):
    tm = 512 if geom.t % 512 == 0 else TB
    return pl.pallas_call(
        _dt_kernel,
        out_shape=[jax.ShapeDtypeStruct((geom.t, 128), F32)] * 2,
        grid=(geom.t // tm,),
        in_specs=[pl.BlockSpec((tm, 128), lambda i: (i, 0)),
                  pl.BlockSpec((1, 128), lambda i: (0, 0)),
                  pl.BlockSpec((1, 128), lambda i: (0, 0))],
        out_specs=[pl.BlockSpec((tm, 128), lambda i: (i, 0))] * 2,
        compiler_params=_cp(("parallel",)),
        name="ssd_dt",
    )(u_dt, dt_bias128, neg_a128)


def _qk_kernel(q_ref, k_ref, cos_ref, sin_ref, qo_ref, kto_ref):
    cos, sin = cos_ref[...], sin_ref[...]

    def rope(x):
        halves = [pltpu.roll(x[:, a:a + 128], 64, 1) for a in (0, 128)]
        return x * cos + jnp.concatenate(halves, axis=1) * sin

    qo_ref[...] = rope(q_ref[...].astype(F32)).astype(BF16)
    kto_ref[...] = (rope(k_ref[...].astype(F32)) * (DK_RET ** -0.5)).T.astype(BF16)


def qk_prep(geom, u, cos_tab, sin_tab):
    tab = lambda i, h: (jnp.where(i < geom.npt, geom.tps, (i - geom.npt) % geom.tps), 0)
    return pl.pallas_call(
        _qk_kernel,
        out_shape=[jax.ShapeDtypeStruct((geom.t, H_RET * DK_RET), BF16),
                   jax.ShapeDtypeStruct((H_RET * DK_RET, geom.t), BF16)],
        grid=(geom.nt, H_RET),
        in_specs=[pl.BlockSpec((TB, DK_RET), lambda i, h: (i, OFF_Q // DK_RET + h)),
                  pl.BlockSpec((TB, DK_RET), lambda i, h: (i, OFF_K // DK_RET + h)),
                  pl.BlockSpec((TB, DK_RET), tab),
                  pl.BlockSpec((TB, DK_RET), tab)],
        out_specs=[pl.BlockSpec((TB, DK_RET), lambda i, h: (i, h)),
                   pl.BlockSpec((DK_RET, TB), lambda i, h: (h, i))],
        compiler_params=_cp(("parallel", "parallel")),
        name="qk_prep",
    )(u, u, cos_tab, sin_tab)


def _scan_kernel(r, p, n, nc, reverse, has_dt, has_init, has_alias, *refs):
    it = iter(refs)
    x_ref, c_ref, bt_ref, la_ref, lat_ref = next(it), next(it), next(it), next(it), next(it)
    dt_ref = next(it) if has_dt else None
    s0_ref = next(it) if has_init else None
    if has_alias:
        next(it)
    y_ref, sf_ref, s_ref, yd_ref = next(it), next(it), next(it), next(it)
    c = pl.program_id(2)
    q = TB

    @pl.when(c == 0)
    def _():
        s_ref[...] = s0_ref[...] if has_init else jnp.zeros_like(s_ref)

    ii = lax.broadcasted_iota(jnp.int32, (q, q), 0)
    jj = lax.broadcasted_iota(jnp.int32, (q, q), 1)
    lower = ii >= jj
    tri_a = (jj >= ii) if reverse else lower
    tri_a_bf = tri_a.astype(BF16)
    tri_b_bf = (lower if reverse else (jj >= ii)).astype(BF16)
    end = 0 if reverse else q - 1

    la_col = la_ref[...]
    la_row = lat_ref[...]
    if r == 1:
        la_col = jnp.broadcast_to(la_col, (q, 128))
        la_row = jnp.broadcast_to(la_row, (8, q))
    cs_col = _dot2_r(tri_a_bf, la_col)[:, :r]
    cs_row = _dot2_l(la_row, tri_b_bf)[:r, :]
    tot = cs_col[end:end + 1, :]
    end_decay = jnp.exp(tot - cs_col)
    in_scale = jnp.exp(cs_col)

    if r == 1:
        expand = lambda a: jnp.broadcast_to(a, (q, p))
    else:
        er = lax.broadcasted_iota(jnp.int32, (r, r * p), 0)
        ec = lax.broadcasted_iota(jnp.int32, (r, r * p), 1)
        e_bf = (ec // p == er).astype(BF16)

        expand = lambda a: _dot2_l(a, e_bf)

    x = x_ref[...].astype(F32)
    if has_dt:
        dt = dt_ref[...]
        xd = x * expand(dt)
        xe = x * expand(dt * end_decay)
    else:
        xd = x
        xe = x * expand(end_decay)
    xd_bf, xe_bf = xd.astype(BF16), xe.astype(BF16)
    c_bf = c_ref[...].astype(BF16)
    bt_bf = bt_ref[...].astype(BF16)

    scores = _dot(c_bf, bt_bf)
    for h in range(r):
        dmat = cs_col[:, h:h + 1] - cs_row[h:h + 1, :]
        wgt = scores * jnp.exp(jnp.where(tri_a, dmat, -1e30))
        yd_ref[:, h * p:(h + 1) * p] = _dot(wgt.astype(BF16), xd_bf[:, h * p:(h + 1) * p])

    in_scale_x = expand(in_scale)
    s_old = s_ref[...]
    y_ref[...] = yd_ref[...] + _dot(c_bf, s_old.astype(BF16)) * in_scale_x
    s_new = s_old * in_scale_x[end:end + 1, :] + _dot(bt_bf, xe_bf)
    s_ref[...] = s_new

    @pl.when(c == nc - 1)
    def _():
        sf_ref[...] = s_new


def chunk_scan(x, xcol0, cm, ccol0, bt, la, lat, dt, s0, y_prev, *, tile0, nseq, nc, g, r, p, n, reverse):
    rp = r * p
    xb, cb = xcol0 // rp, ccol0 // n
    ch = (lambda c: nc - 1 - c) if reverse else (lambda c: c)
    tok = lambda b, c: tile0 + b * nc + ch(c)
    in_specs = [pl.BlockSpec((TB, rp), lambda b, gi, c: (tok(b, c), xb + gi)),
                pl.BlockSpec((TB, n), lambda b, gi, c: (tok(b, c), cb + gi)),
                pl.BlockSpec((n, TB), lambda b, gi, c: (gi, tok(b, c))),
                pl.BlockSpec((None, TB, r), lambda b, gi, c: (gi, tok(b, c), 0)),
                pl.BlockSpec((None, r, TB), lambda b, gi, c: (gi, 0, tok(b, c)))]
    args = [x, cm, bt, la, lat]
    if dt is not None:
        in_specs.append(pl.BlockSpec((None, TB, r), lambda b, gi, c: (gi, tok(b, c), 0)))
        args.append(dt)
    if s0 is not None:
        in_specs.append(pl.BlockSpec((None, None, n, rp), lambda b, gi, c: (b, gi, 0, 0)))
        args.append(s0)
    aliases = {}
    if y_prev is not None:
        in_specs.append(pl.BlockSpec(memory_space=pl.ANY))
        args.append(y_prev)
        aliases = {len(args) - 1: 0}
    return pl.pallas_call(
        functools.partial(_scan_kernel, r, p, n, nc, reverse, dt is not None, s0 is not None, y_prev is not None),
        out_shape=[jax.ShapeDtypeStruct((x.shape[0], g * rp), F32),
                   jax.ShapeDtypeStruct((nseq, g, n, rp), F32)],
        grid=(nseq, g, nc),
        in_specs=in_specs,
        out_specs=[pl.BlockSpec((TB, rp), lambda b, gi, c: (tok(b, c), gi)),
                   pl.BlockSpec((None, None, n, rp), lambda b, gi, c: (b, gi, 0, 0))],
        scratch_shapes=[pltpu.VMEM((n, rp), F32), pltpu.VMEM((TB, rp), F32)],
        input_output_aliases=aliases,
        compiler_params=_cp(("parallel", "parallel", "arbitrary")),
        name="chunk_scan",
    )(*args)


def _filt_hidden_kernel(ls, freq_ref, w1a_ref, w1c_ref, w1s_ref, b1_ref, w2t_ref, b2_ref, sf_ref, o_ref):
    pp = lax.broadcasted_iota(jnp.int32, (1, 2 * ls), 1)
    t = jnp.abs(pp - ls).astype(F32)
    t_norm = t / max(ls - 1, 1)
    ang = (2.0 * math.pi) * t * freq_ref[...] / ls
    pre = (w1a_ref[...] * t_norm
           + jnp.dot(w1c_ref[...], jnp.cos(ang), precision=HIGHEST, preferred_element_type=F32)
           - jnp.dot(w1s_ref[...], jnp.sin(ang), precision=HIGHEST, preferred_element_type=F32)
           + b1_ref[...])
    sf = sf_ref[...]
    h1 = jnp.sin(sf[:, 0:1] * pre)
    h2 = jnp.sin(sf[:, 1:2] * (jnp.dot(w2t_ref[...], h1, precision=HIGHEST, preferred_element_type=F32)
                               + b2_ref[...]))
    o_ref[...] = h2


def _filt_out_kernel(ls, hid_ref, wf_ref, wb_ref, delta_ref, bias_ref, o_ref):
    pp = lax.broadcasted_iota(jnp.int32, (1, 2 * ls), 1)
    t_norm = jnp.abs(pp - ls).astype(F32) / max(ls - 1, 1)
    hid = hid_ref[...]
    kf = jnp.dot(wf_ref[...], hid, precision=HIGHEST, preferred_element_type=F32)
    kb = jnp.dot(wb_ref[...], hid, precision=HIGHEST, preferred_element_type=F32)
    window = jnp.exp(-t_norm * jnp.abs(delta_ref[...]))
    k = jnp.where(pp >= ls, kf, kb) * window
    k = jnp.where(pp == 0, 0.0, k)
    o_ref[...] = k + jnp.where(pp == ls, bias_ref[...], 0.0)


def hyena_filter_rows(ls, hy_w1, hy_b1, hy_w2, hy_b2, hy_w3, hy_freq, hy_bias):
    bands = (POS_EMB - 1) // 2
    freqs = jnp.linspace(1e-4, bands - 1, bands, dtype=F32).reshape(bands, 1)
    w1t = jnp.swapaxes(hy_w1, 1, 2)
    hid = pl.pallas_call(
        functools.partial(_filt_hidden_kernel, ls),
        out_shape=jax.ShapeDtypeStruct((DEPTH, FILT_HID, 2 * ls), F32),
        grid=(DEPTH,),
        in_specs=[pl.BlockSpec((bands, 1), lambda l: (0, 0)),
                  pl.BlockSpec((None, FILT_HID, 1), lambda l: (l, 0, 0)),
                  pl.BlockSpec((None, FILT_HID, bands), lambda l: (l, 0, 0)),
                  pl.BlockSpec((None, FILT_HID, bands), lambda l: (l, 0, 0)),
                  pl.BlockSpec((None, FILT_HID, 1), lambda l: (l, 0, 0)),
                  pl.BlockSpec((None, FILT_HID, FILT_HID), lambda l: (l, 0, 0)),
                  pl.BlockSpec((None, FILT_HID, 1), lambda l: (l, 0, 0)),
                  pl.BlockSpec((None, FILT_HID, 2), lambda l: (l, 0, 0))],
        out_specs=pl.BlockSpec((None, FILT_HID, 2 * ls), lambda l: (l, 0, 0)),
        compiler_params=_cp(("parallel",)),
        name="hyena_filter_hidden",
    )(freqs, w1t[:, :, 0:1], w1t[:, :, 1:1 + bands], w1t[:, :, 1 + bands:], hy_b1[..., None],
      jnp.swapaxes(hy_w2, 1, 2), hy_b2[..., None], jnp.swapaxes(hy_freq, 1, 2))
    w3t = jnp.swapaxes(hy_w3, 1, 2).reshape(DEPTH, 2, 2, W_HY, FILT_HID)
    deltas = jnp.linspace(math.log(HY_TARGET) / HY_SLOW, math.log(HY_TARGET) / HY_FAST, W_HY,
                          dtype=F32).reshape(W_HY, 1)
    cb = 256
    return pl.pallas_call(
        functools.partial(_filt_out_kernel, ls),
        out_shape=jax.ShapeDtypeStruct((DEPTH, 2, W_HY, 2 * ls), F32),
        grid=(DEPTH, 2, W_HY // cb),
        in_specs=[pl.BlockSpec((None, FILT_HID, 2 * ls), lambda l, o, j: (l, 0, 0)),
                  pl.BlockSpec((None, None, None, cb, FILT_HID), lambda l, o, j: (l, o, 0, j, 0)),
                  pl.BlockSpec((None, None, None, cb, FILT_HID), lambda l, o, j: (l, o, 1, j, 0)),
                  pl.BlockSpec((cb, 1), lambda l, o, j: (j, 0)),
                  pl.BlockSpec((None, None, cb, 1), lambda l, o, j: (l, o, j, 0))],
        out_specs=pl.BlockSpec((None, None, cb, 2 * ls), lambda l, o, j: (l, o, j, 0)),
        compiler_params=_cp(("parallel", "parallel", "parallel")),
        name="hyena_filter_rows",
    )(hid, w3t, w3t, deltas, hy_bias[..., None])


def _hyena_kernel(nb, b, cbk, v_ref, x1_ref, x2_ref, w_ref, o_ref, acc_ref):
    ls2 = 2 * nb * TB

    def conv(u, o, ch):
        wrow = w_ref[o, pl.ds(ch, 1), :]
        toep = pltpu.roll(jnp.broadcast_to(wrow, (TB, ls2)), 0, 1, stride=1, stride_axis=0).astype(BF16)
        acc_ref[...] = jnp.zeros_like(acc_ref)
        for mm in range(2 * nb - 1):
            m = mm - (nb - 1)
            lo, hi = max(0, -m), min(nb, nb - m)
            part = _dot(u[lo * b:hi * b, :].astype(BF16), toep[:, TB * (mm + 1):TB * (mm + 2)])
            acc_ref[(lo + m) * b:(hi + m) * b, :] += part
        return acc_ref[...]

    def body(ch, carry):
        z = x1_ref[ch] * conv(v_ref[ch], 0, ch)
        o_ref[ch] = x2_ref[ch] * conv(z, 1, ch)
        return carry

    lax.fori_loop(0, cbk, body, 0, unroll=2)


def hyena_conv(uct, filt_l, nb, b):
    cbk = 8
    rows = nb * b
    return pl.pallas_call(
        functools.partial(_hyena_kernel, nb, b, cbk),
        out_shape=jax.ShapeDtypeStruct((W_HY, rows, TB), F32),
        grid=(W_HY // cbk,),
        in_specs=[pl.BlockSpec((None, cbk, rows, TB), lambda j: (0, j, 0, 0)),
                  pl.BlockSpec((None, cbk, rows, TB), lambda j: (1, j, 0, 0)),
                  pl.BlockSpec((None, cbk, rows, TB), lambda j: (2, j, 0, 0)),
                  pl.BlockSpec((2, cbk, 2 * nb * TB), lambda j: (0, j, 0))],
        out_specs=pl.BlockSpec((cbk, rows, TB), lambda j: (j, 0, 0)),
        scratch_shapes=[pltpu.VMEM((rows, TB), F32)],
        compiler_params=_cp(("parallel",)),
        name="hyena_conv",
    )(uct, uct, uct, filt_l)


def _ssd_post_kernel(yf_ref, yb_ref, xs_ref, z_ref, d_ref, nw_ref, o_ref):
    y = yf_ref[...] + yb_ref[...] + xs_ref[...] * d_ref[...]
    y = y * _silu(z_ref[...].astype(F32))
    y = y * lax.rsqrt(jnp.mean(y * y, axis=-1, keepdims=True) + LN_EPS)
    o_ref[...] = (y * nw_ref[...]).astype(o_ref.dtype)


def ssd_post(geom, yf, yb, xs, u, d_exp, norm_w):
    gw = D_INNER // G_SSD
    tm = 512 if geom.t % 512 == 0 else TB
    blk = pl.BlockSpec((tm, gw), lambda i, gi: (i, gi))
    vec = pl.BlockSpec((1, gw), lambda i, gi: (0, gi))
    return pl.pallas_call(
        _ssd_post_kernel,
        out_shape=jax.ShapeDtypeStruct((geom.t, D_INNER), BF16),
        grid=(geom.t // tm, G_SSD),
        in_specs=[blk, blk, blk, pl.BlockSpec((tm, gw), lambda i, gi: (i, OFF_Z // gw + gi)), vec, vec],
        out_specs=blk,
        compiler_params=_cp(("parallel", "parallel")),
        name="ssd_post",
    )(yf, yb, xs, u, d_exp, norm_w)


def _ret_post_kernel(of_ref, ob_ref, g_ref, w_ref, o_ref):
    o = _ln(of_ref[...] + ob_ref[...]) * w_ref[...]
    o_ref[...] = (o * _silu(g_ref[...].astype(F32))).astype(o_ref.dtype)


def ret_post(geom, of, ob, u, gn_w):
    tm = 512 if geom.t % 512 == 0 else TB
    blk = pl.BlockSpec((tm, DV_RET), lambda i, h: (i, h))
    return pl.pallas_call(
        _ret_post_kernel,
        out_shape=jax.ShapeDtypeStruct((geom.t, H_RET * DV_RET), BF16),
        grid=(geom.t // tm, H_RET),
        in_specs=[blk, blk, pl.BlockSpec((tm, DV_RET), lambda i, h: (i, OFF_G // DV_RET + h)),
                  pl.BlockSpec((1, DV_RET), lambda i, h: (0, h))],
        out_specs=blk,
        compiler_params=_cp(("parallel", "parallel")),
        name="ret_post",
    )(of, ob, u, gn_w)


def _merge_kernel(yhy_ref, yssd_ref, yret_ref, mg0_ref, mg1_ref, mg2_ref, x_ref, g1_ref, sh2_ref, sc2_ref,
                  lg_ref, lb_ref, whp_ref, wsp_ref, wrp_ref, wo_ref, rwt_ref, rb_ref,
                  x1_ref, h_ref, aff_ref):
    gate = lambda r: jax.nn.sigmoid(r[...].astype(F32))
    merged = (gate(mg0_ref) * _dot(yhy_ref[...].astype(BF16), whp_ref[...])
              + gate(mg1_ref) * _dot(yssd_ref[...], wsp_ref[...])
              + gate(mg2_ref) * _dot(yret_ref[...], wrp_ref[...]))
    mix = _dot(merged.astype(BF16), wo_ref[...])
    x1 = _ln(ALPHA * x_ref[...] + g1_ref[...] * mix) * lg_ref[...] + lb_ref[...]
    x1_ref[...] = x1
    h = _ln(x1) * (1.0 + sc2_ref[...]) + sh2_ref[...]
    h_ref[...] = h.astype(BF16)
    logits = lax.dot_general(rwt_ref[...], h, (((1,), (1,)), ((), ())), precision=HIGHEST,
                             preferred_element_type=F32) + rb_ref[...]
    e = jnp.exp(logits - jnp.max(logits, axis=0, keepdims=True))
    aff_ref[...] = e / jnp.sum(e, axis=0, keepdims=True)


def merge(geom, l, y_hy, y_ssd, y_ret, u, x, mod, ln_g, ln_b, whp, wsp, wrp, wo, rwt, rb):
    tm = TB
    row = geom.mod_row
    tok = lambda w: pl.BlockSpec((tm, w), lambda i: (i, 0))
    mgs = [pl.BlockSpec((tm, D), lambda i, k=k: (i, OFF_MG // D + k)) for k in range(3)]
    modspec = lambda k: pl.BlockSpec((None, None, None, 1, D), lambda i: (l, k, row(i), 0, 0))
    vec = pl.BlockSpec((None, 1, D), lambda i: (l, 0, 0))
    wspec = lambda kk: pl.BlockSpec((None, kk, D), lambda i: (l, 0, 0))
    return pl.pallas_call(
        _merge_kernel,
        out_shape=[jax.ShapeDtypeStruct((geom.t, D), F32), jax.ShapeDtypeStruct((geom.t, D), BF16),
                   jax.ShapeDtypeStruct((N_EXPERTS, geom.t), F32)],
        grid=(geom.t // tm,),
        in_specs=[tok(D), tok(D_INNER), tok(D)] + mgs + [tok(D), modspec(2), modspec(3), modspec(4), vec, vec,
                  wspec(D), wspec(D_INNER), wspec(D), wspec(D),
                  pl.BlockSpec((None, N_EXPERTS, D), lambda i: (l, 0, 0)),
                  pl.BlockSpec((None, N_EXPERTS, 1), lambda i: (l, 0, 0))],
        out_specs=[tok(D), tok(D), pl.BlockSpec((N_EXPERTS, tm), lambda i: (0, i))],
        compiler_params=_cp(("parallel",)),
        name="merge",
    )(y_hy, y_ssd, y_ret, u, u, u, x, mod, mod, mod, ln_g, ln_b, whp, wsp, wrp, wo, rwt, rb)


def _select_kernel(n, cap, aff_ref, slot_ref, bnd_ref):
    aff = aff_ref[...]
    bits = pltpu.bitcast(aff, jnp.int32)

    def bit_step(k, thr):
        cand = thr | jnp.left_shift(jnp.int32(1), 30 - k)
        cnt = jnp.sum(jnp.where(bits >= cand, 1.0, 0.0), axis=1, keepdims=True)
        return jnp.where(cnt >= cap, cand, thr)

    thr = lax.fori_loop(0, 31, bit_step, jnp.zeros((N_EXPERTS, 1), jnp.int32))
    gt = bits > thr
    eq = bits == thr
    need = cap - jnp.sum(jnp.where(gt, 1.0, 0.0), axis=1, keepdims=True)

    kk = lax.broadcasted_iota(jnp.int32, (TB, TB), 0)
    jj = lax.broadcasted_iota(jnp.int32, (TB, TB), 1)
    before = (kk < jj).astype(BF16)

    def excl_cumsum(mask):
        ones = jnp.where(mask, 1.0, 0.0)
        outs, carry = [], jnp.zeros((N_EXPERTS, 1), F32)
        for j in range(n // TB):
            blk = ones[:, j * TB:(j + 1) * TB]
            outs.append(_dot(blk.astype(BF16), before) + carry)
            carry = carry + jnp.sum(blk, axis=1, keepdims=True)
        return jnp.concatenate(outs, axis=1)

    sel = gt | (eq & (excl_cumsum(eq) < need))
    slot_ref[...] = jnp.where(sel, excl_cumsum(sel), -1.0)
    tt = lax.broadcasted_iota(jnp.int32, (n, 128), 0)
    tj = lax.broadcasted_iota(jnp.int32, (n, 128), 1)
    bnd_ref[...] = _dot(jnp.where(sel, 1.0, 0.0).astype(BF16), (tt < tj * TB).astype(BF16))


def ec_select(aff_t, tile0, nseq, n):
    cap = EC_CAPACITY * n // N_EXPERTS
    nt = n // TB
    assert tile0 % nt == 0 and nt < 128
    return pl.pallas_call(
        functools.partial(_select_kernel, n, cap),
        out_shape=[jax.ShapeDtypeStruct((N_EXPERTS, nseq * n), F32),
                   jax.ShapeDtypeStruct((nseq, N_EXPERTS, 128), F32)],
        grid=(nseq,),
        in_specs=[pl.BlockSpec((N_EXPERTS, n), lambda s: (0, tile0 // nt + s))],
        out_specs=[pl.BlockSpec((N_EXPERTS, n), lambda s: (0, s)),
                   pl.BlockSpec((None, N_EXPERTS, 128), lambda s: (s, 0, 0))],
        compiler_params=_cp(("parallel",)),
        name="ec_select",
    )(aff_t)


def _moe_kernel(nd, ntd, cap, sc, has_alias, b_ref, slot_ref, aff_ref, h_ref, w1_ref, w3_ref, w2_ref, *rest):
    o_ref, xe_ref, g_ref, ye_ref = rest[1:] if has_alias else rest
    blk, e = pl.program_id(0), pl.program_id(1)

    @pl.when(e == 0)
    def _():
        o_ref[...] = jnp.zeros_like(o_ref)

    xe_ref[...] = jnp.zeros_like(xe_ref)
    g_ref[...] = jnp.zeros_like(g_ref)

    def for_pairs(fn):
        def tile_body(t, carry):
            d = 0 if nd == 1 else (t if ntd == 1 else t // ntd)
            slot = slot_ref[e, pl.ds(t, 1), :]
            tok0 = pl.multiple_of(t * TB, TB)
            if ntd > 1:
                base = ((blk * nd + d) * N_EXPERTS + e) * BOUNDS_STRIDE + (t - d * ntd)
                lo, hi = b_ref[base], b_ref[base + 1]
            for c in range(cap // sc):
                row0 = d * cap + c * sc
                if nd > 1:
                    row0 = pl.multiple_of(row0, sc)

                def run(c=c, row0=row0):
                    jcol = (lax.broadcasted_iota(jnp.int32, (sc, 1), 0) + c * sc).astype(F32)
                    fn(t, tok0, row0, slot == jcol)

                if ntd > 1:
                    pl.when((hi > c * sc) & (lo < (c + 1) * sc))(run)
                else:
                    run()
            return carry

        lax.fori_loop(0, nd * ntd, tile_body, 0)

    def gather(t, tok0, row0, pick):
        pick_bf = jnp.where(pick, 1.0, 0.0).astype(BF16)
        gate = aff_ref[e, pl.ds(t, 1), :]
        xe_ref[pl.ds(row0, sc), :] += _dot(pick_bf, h_ref[pl.ds(tok0, TB), :])
        g_ref[pl.ds(row0, sc), :] += jnp.sum(jnp.where(pick, gate, 0.0), axis=1, keepdims=True)

    for_pairs(gather)

    xe = xe_ref[...].astype(BF16)
    hid = _silu(_dot(xe, w1_ref[...])) * _dot(xe, w3_ref[...]) * g_ref[...]
    ye_ref[...] = _dot(hid.astype(BF16), w2_ref[...]).astype(BF16)

    def scatter(t, tok0, row0, pick):
        pick_bf = jnp.where(pick, 1.0, 0.0).astype(BF16)
        o_ref[pl.ds(tok0, TB), :] += lax.dot_general(pick_bf, ye_ref[pl.ds(row0, sc), :], (((0,), (0,)), ((), ())),
                                                     preferred_element_type=F32)

    for_pairs(scatter)


MOE_BLOCK_TOKENS = 4096
BOUNDS_STRIDE = 32


def moe_experts(l, slot, bounds, aff_t, h, w1, w3, w2, ffn_prev, tile0, nseq, n):
    cap = EC_CAPACITY * n // N_EXPERTS
    ntd = n // TB
    nd = max(d for d in range(1, nseq + 1) if nseq % d == 0 and d * n <= MOE_BLOCK_TOKENS)
    tpb = nd * ntd
    assert tile0 % tpb == 0 and tpb % 8 == 0
    sc = min(cap, 128)
    nslot = nd * cap
    slot3 = slot.reshape(N_EXPERTS, nseq * ntd, TB)
    aff3 = aff_t.reshape(N_EXPERTS, aff_t.shape[1] // TB, TB)
    assert ntd + 1 <= BOUNDS_STRIDE
    bflat = (bounds[:, :, :BOUNDS_STRIDE].astype(jnp.int32).reshape(-1) if ntd > 1
             else jnp.zeros((8,), jnp.int32))
    wspec = pl.BlockSpec((None, None, D, D), lambda s, e, b: (l, e, 0, 0))
    in_specs = [pl.BlockSpec((N_EXPERTS, tpb, TB), lambda s, e, b: (0, s, 0)),
                pl.BlockSpec((N_EXPERTS, tpb, TB), lambda s, e, b: (0, tile0 // tpb + s, 0)),
                pl.BlockSpec((tpb * TB, D), lambda s, e, b: (tile0 // tpb + s, 0), pipeline_mode=pl.Buffered(1)),
                wspec, wspec, wspec]
    args = [bflat, slot3, aff3, h, w1, w3, w2]
    aliases = {}
    if ffn_prev is not None:
        in_specs.append(pl.BlockSpec(memory_space=pl.ANY))
        args.append(ffn_prev)
        aliases = {len(args) - 1: 0}
    grid_spec = pltpu.PrefetchScalarGridSpec(
        num_scalar_prefetch=1,
        grid=(nseq // nd, N_EXPERTS),
        in_specs=in_specs,
        out_specs=pl.BlockSpec((tpb * TB, D), lambda s, e, b: (tile0 // tpb + s, 0), pipeline_mode=pl.Buffered(1)),
        scratch_shapes=[pltpu.VMEM((nslot, D), F32), pltpu.VMEM((nslot, 1), F32), pltpu.VMEM((nslot, D), BF16)])
    return pl.pallas_call(
        functools.partial(_moe_kernel, nd, ntd, cap, sc, ffn_prev is not None),
        out_shape=jax.ShapeDtypeStruct((h.shape[0], D), F32),
        grid_spec=grid_spec,
        input_output_aliases=aliases,
        compiler_params=_cp(("parallel", "arbitrary")),
        name="moe_experts",
    )(*args)


def _resid_kernel(x_ref, f_ref, g_ref, lg_ref, lb_ref, o_ref):
    o_ref[...] = _ln(ALPHA * x_ref[...] + g_ref[...] * f_ref[...]) * lg_ref[...] + lb_ref[...]


def resid_ln(geom, l, x1, ffn, mod, ln_g, ln_b):
    tm = 512 if (geom.tp % 512 == 0 and geom.ls % 512 == 0) else TB
    row = functools.partial(geom.mod_row, rows_per_tile=tm)
    tok = pl.BlockSpec((tm, D), lambda i: (i, 0))
    vec = pl.BlockSpec((None, 1, D), lambda i: (l, 0, 0))
    return pl.pallas_call(
        _resid_kernel,
        out_shape=jax.ShapeDtypeStruct((geom.t, D), F32),
        grid=(geom.t // tm,),
        in_specs=[tok, tok, pl.BlockSpec((None, None, None, 1, D), lambda i: (l, 5, row(i), 0, 0)), vec, vec],
        out_specs=tok,
        compiler_params=_cp(("parallel",)),
        name="resid_ln",
    )(x1, ffn, mod, ln_g, ln_b)


def _rope_tables(ls):
    quarter = DK_RET // 4
    inv_freq = ROPE_BASE ** (-jnp.arange(quarter, dtype=F32) / quarter)
    pos = jnp.arange(ls, dtype=jnp.int32)
    row_ang = (pos // GRID_W).astype(F32)[:, None] * inv_freq[None]
    col_ang = (pos % GRID_W).astype(F32)[:, None] * inv_freq[None]
    cos = jnp.concatenate([jnp.cos(row_ang)] * 2 + [jnp.cos(col_ang)] * 2, axis=1)
    sin = jnp.concatenate([-jnp.sin(row_ang), jnp.sin(row_ang), -jnp.sin(col_ang), jnp.sin(col_ang)], axis=1)
    cos = jnp.concatenate([cos, jnp.ones((TB, DK_RET), F32)], axis=0)
    sin = jnp.concatenate([sin, jnp.zeros((TB, DK_RET), F32)], axis=0)
    return cos, sin


def _to_channel_major(uc, b, nb):
    x = uc.reshape(b, nb, TB, 3, W_HY)
    return jnp.transpose(x, (3, 4, 1, 0, 2)).reshape(3, W_HY, nb * b, TB)


def _to_token_major(yt, b, nb):
    x = yt.reshape(W_HY, nb, b, TB)
    return jnp.transpose(x, (2, 1, 3, 0)).reshape(b * nb * TB, W_HY)


def _state_to_t(s, g, r):
    b, h, p, n = s.shape
    return jnp.transpose(s.reshape(b, g, r, p, n), (0, 1, 4, 2, 3)).reshape(b, g, n, r * p)


def _state_from_t(st, r):
    b, g, n, rp = st.shape
    p = rp // r
    return jnp.transpose(st.reshape(b, g, n, r, p), (0, 1, 3, 4, 2)).reshape(b, g * r, p, n)


def _per_group(a, g, r):
    t = a.shape[0]
    a3 = a.reshape(t, g, r)
    return jnp.transpose(a3, (1, 0, 2)), jnp.transpose(a3, (1, 2, 0))


def kernel(x_prompt, x_sample, c, state_ssd, state_ret, c_ctx, w_in, ada_w, ada_b, hy_conv_w, hy_conv_b, hy_w1, hy_b1, hy_w2, hy_b2, hy_w3, hy_freq, hy_bias, hy_proj, ssd_conv_w, ssd_conv_b, ssd_dt_bias, ssd_a_log, ssd_d, ssd_norm_w, ssd_proj, ret_decay, ret_gn_w, ret_proj, w_out, ln1_g, ln1_b, ln2_g, ln2_b, router_w, router_b, exp_w1, exp_w3, exp_w2):
    bp, lp, _ = x_prompt.shape
    bs, ls, _ = x_sample.shape
    geom = Geom(bp, lp, bs, ls)
    assert lp % TB == 0 and ls % TB == 0 and bs + 1 <= 16 and bp % 8 == 0 and bs % 8 == 0 and (bp * lp) % ls == 0

    w_in_p = jnp.concatenate([w_in[:, :, :OFF_Q], w_in[:, :, OFF_Q + 64:], w_in[:, :, OFF_Q:OFF_Q + 64],
                              jnp.zeros((DEPTH, D, P_PAD - OFF_DT - 64), F32)], axis=2).astype(BF16)
    whp, wsp, wrp, wo = (w.astype(BF16) for w in (hy_proj, ssd_proj, ret_proj, w_out))
    ew1, ew3, ew2 = (w.astype(BF16) for w in (exp_w1, exp_w3, exp_w2))
    rwt = jnp.swapaxes(router_w, 1, 2)
    rb = router_b[..., None]
    vec3 = lambda a: a[:, None, :]
    cond16 = jnp.zeros((16, D), F32).at[0].set(c_ctx).at[1:1 + bs].set(c)
    mod = ada_mod(cond16, ada_w, ada_b)
    mod = jnp.transpose(mod.reshape(DEPTH, 16, 6, D), (0, 2, 1, 3))[:, :, :, None, :]

    filt_p = hyena_filter_rows(lp, hy_w1, hy_b1, hy_w2, hy_b2, hy_w3, hy_freq, hy_bias)
    filt_s = hyena_filter_rows(ls, hy_w1, hy_b1, hy_w2, hy_b2, hy_w3, hy_freq, hy_bias)
    cos_tab, sin_tab = _rope_tables(ls)
    pad64 = lambda a: jnp.pad(a.reshape(1, 2 * H_SSD), ((0, 0), (0, 128 - 2 * H_SSD)))
    d_exp = jnp.repeat(ssd_d, P_SSD, axis=1)
    ret_la = -jnp.exp(ret_decay)

    x = jnp.concatenate([x_prompt.reshape(geom.tp, D), x_sample.reshape(geom.ts, D)], axis=0)
    ssd_states, ret_states = [], []
    groups = ((0, bp, geom.tpp), (geom.npt, bs, geom.tps))

    for l in range(DEPTH):
        u, u_dt = in_proj(geom, x, mod, l, w_in_p)

        uc_hy = dwconv3(geom, u, OFF_HY, hy_conv_w[l], hy_conv_b[l][None], 0, 3 * W_HY, act=False, cw=1024)
        y_parts = []
        for (tile0, nseq, nc), filt in zip(groups, (filt_p, filt_s)):
            rows = slice(tile0 * TB, (tile0 + nseq * nc) * TB)
            yt = hyena_conv(_to_channel_major(uc_hy[rows], nseq, nc), filt[l], nc, nseq)
            y_parts.append(_to_token_major(yt, nseq, nc))
        y_hy = jnp.concatenate(y_parts, axis=0)

        xs = dwconv3(geom, u, OFF_XBC, ssd_conv_w[l], ssd_conv_b[l][None], 0, D_INNER, act=True, cw=1024)
        bm_t = dwconv3(geom, u, OFF_XBC + D_INNER, ssd_conv_w[l], ssd_conv_b[l][None], D_INNER,
                       G_SSD * N_SSD, act=True, out_dtype=BF16, transpose=True)
        cm = dwconv3(geom, u, OFF_XBC + D_INNER + G_SSD * N_SSD, ssd_conv_w[l], ssd_conv_b[l][None],
                     D_INNER + G_SSD * N_SSD, G_SSD * N_SSD, act=True, out_dtype=BF16)
        dt, la = ssd_dt(geom, u_dt, pad64(ssd_dt_bias[l]), pad64(-jnp.exp(ssd_a_log[l])))
        y_ssd_dir, s_ssd_dir = [], []
        for d in range(2):
            la_g, la_gt = _per_group(la[:, d * H_SSD:(d + 1) * H_SSD], G_SSD, R_SSD)
            dt_g, _ = _per_group(dt[:, d * H_SSD:(d + 1) * H_SSD], G_SSD, R_SSD)
            y = None
            for gi, (tile0, nseq, nc) in enumerate(groups):
                s0 = None if gi == 0 else _state_to_t(state_ssd[:, l, d], G_SSD, R_SSD)
                y, sf = chunk_scan(xs, 0, cm, 0, bm_t, la_g, la_gt, dt_g, s0, y, tile0=tile0, nseq=nseq, nc=nc,
                                   g=G_SSD, r=R_SSD, p=P_SSD, n=N_SSD, reverse=(d == 1))
                if gi == 0:
                    s_ssd_dir.append(_state_from_t(sf, R_SSD))
            y_ssd_dir.append(y)
        ssd_states.append(jnp.stack(s_ssd_dir, axis=1))
        y_ssd = ssd_post(geom, y_ssd_dir[0], y_ssd_dir[1], xs, u, d_exp[l][None], ssd_norm_w[l][None])

        q_r, k_t = qk_prep(geom, u, cos_tab, sin_tab)
        y_ret_dir, s_ret_dir = [], []
        for d in range(2):
            la_tok = jnp.broadcast_to(ret_la[l, d][None, :], (geom.t, H_RET))
            la_g, la_gt = _per_group(la_tok, H_RET, 1)
            y = None
            for gi, (tile0, nseq, nc) in enumerate(groups):
                s0 = None if gi == 0 else _state_to_t(state_ret[:, l, d], H_RET, 1)
                y, sf = chunk_scan(u, OFF_V, q_r, 0, k_t, la_g, la_gt, None, s0, y, tile0=tile0, nseq=nseq, nc=nc,
                                   g=H_RET, r=1, p=DV_RET, n=DK_RET, reverse=(d == 1))
                if gi == 0:
                    s_ret_dir.append(_state_from_t(sf, 1))
            y_ret_dir.append(y)
        ret_states.append(jnp.stack(s_ret_dir, axis=1))
        y_ret = ret_post(geom, y_ret_dir[0], y_ret_dir[1], u, ret_gn_w[l][None])

        x1, h_ffn, aff_t = merge(geom, l, y_hy, y_ssd, y_ret, u, x, mod, vec3(ln1_g), vec3(ln1_b),
                                 whp, wsp, wrp, wo, rwt, rb)

        ffn = None
        for (tile0, nseq, nc) in groups:
            n = nc * TB
            slot, bounds = ec_select(aff_t, tile0, nseq, n)
            ffn = moe_experts(l, slot, bounds, aff_t, h_ffn, ew1, ew3, ew2, ffn, tile0, nseq, n)
        x = resid_ln(geom, l, x1, ffn, mod, vec3(ln2_g), vec3(ln2_b))

    y_prompt = x[:geom.tp].reshape(bp, lp, D)
    y_sample = x[geom.tp:].reshape(bs, ls, D)
    return (y_prompt, y_sample, jnp.stack(ssd_states, axis=1), jnp.stack(ret_states, axis=1))
```

```python
import functools
import math

import jax
import jax.numpy as jnp
from jax import lax
from jax.experimental import pallas as pl
from jax.experimental.pallas import tpu as pltpu

F32 = jnp.float32
BF16 = jnp.bfloat16
HIGHEST = lax.Precision.HIGHEST

D = 1024
DEPTH = 4
GRID_W = 64
LN_EPS = 1e-5
W_HY = D
POS_EMB = 33
FILT_HID = 64
HY_TARGET, HY_FAST, HY_SLOW = 1e-2, 0.3, 1.5
D_INNER = 2 * D
P_SSD = 64
H_SSD = D_INNER // P_SSD
G_SSD = 4
N_SSD = 128
R_SSD = H_SSD // G_SSD
CONV_CH = D_INNER + 2 * G_SSD * N_SSD
H_RET = 4
DK_RET = 256
DV_RET = 256
ROPE_BASE = 10000.0
N_EXPERTS = 16
EC_CAPACITY = 2
ALPHA = (2 * DEPTH) ** 0.25

OFF_HY, OFF_Z, OFF_XBC, OFF_Q, OFF_K, OFF_V, OFF_G, OFF_MG, OFF_DT = (
    0, 3072, 5120, 8192, 9216, 10240, 11264, 12288, 15360)
P_PAD = 15872

TB = 256
HALO = 16
V7X_VMEM_LIMIT = 56 * 1024 * 1024


def _cp(sem, vmem=None):
    return pltpu.CompilerParams(dimension_semantics=sem, vmem_limit_bytes=vmem or V7X_VMEM_LIMIT)


def _silu(x):
    return x * jax.nn.sigmoid(x)


def _ln(x):
    mu = jnp.mean(x, axis=-1, keepdims=True)
    xc = x - mu
    var = jnp.mean(xc * xc, axis=-1, keepdims=True)
    return xc * lax.rsqrt(var + LN_EPS)


def _split2(a):
    hi = a.astype(BF16)
    return hi, (a - hi.astype(F32)).astype(BF16)


def _dot(a, b):
    return jnp.dot(a, b, preferred_element_type=F32)


def _dot2_l(a_f32, b_bf16):
    hi, lo = _split2(a_f32)
    return _dot(hi, b_bf16) + _dot(lo, b_bf16)


def _dot2_r(a_bf16, b_f32):
    hi, lo = _split2(b_f32)
    return _dot(a_bf16, hi) + _dot(a_bf16, lo)


class Geom:
    def __init__(self, bp, lp, bs, ls):
        self.bp, self.lp, self.bs, self.ls = bp, lp, bs, ls
        self.tp, self.ts = bp * lp, bs * ls
        self.t = self.tp + self.ts
        self.tpp, self.tps = lp // TB, ls // TB
        self.npt, self.nst = self.tp // TB, self.ts // TB
        self.nt = self.npt + self.nst

    def mod_row(self, i, rows_per_tile=TB):
        npt = self.tp // rows_per_tile
        per = self.ls // rows_per_tile
        return jnp.where(i < npt, 0, 1 + (i - npt) // per)


def _ada_kernel(c_ref, w_ref, b_ref, o_ref):
    s = _silu(c_ref[...])
    o_ref[...] = jnp.dot(s, w_ref[...], precision=HIGHEST, preferred_element_type=F32) + b_ref[...]


def ada_mod(cond16, ada_w, ada_b):
    tn = 1536
    return pl.pallas_call(
        _ada_kernel,
        out_shape=jax.ShapeDtypeStruct((DEPTH, 16, 6 * D), F32),
        grid=(DEPTH, 6 * D // tn),
        in_specs=[pl.BlockSpec((16, D), lambda l, j: (0, 0)),
                  pl.BlockSpec((None, D, tn), lambda l, j: (l, 0, j)),
                  pl.BlockSpec((None, 1, tn), lambda l, j: (l, 0, j))],
        out_specs=pl.BlockSpec((None, 16, tn), lambda l, j: (l, 0, j)),
        compiler_params=_cp(("parallel", "parallel")),
        name="ada_mod",
    )(cond16, ada_w, ada_b.reshape(DEPTH, 1, 6 * D))


def _inproj_kernel(nj, x_ref, sh_ref, sc_ref, w_ref, o_ref, dt_ref, h_ref):
    j = pl.program_id(1)

    @pl.when(j == 0)
    def _():
        h = _ln(x_ref[...]) * (1.0 + sc_ref[...]) + sh_ref[...]
        h_ref[...] = h.astype(BF16)

    acc = _dot(h_ref[...], w_ref[...])
    o_ref[...] = acc.astype(o_ref.dtype)

    @pl.when(j == nj - 1)
    def _():
        dt_ref[...] = acc[:, :128]


def in_proj(geom, x, mod, l, w_in_p):
    tm = next(t for t in (2048, 1024, 512, 256) if geom.tp % t == 0 and geom.ls % t == 0)
    tn = 512
    nj = P_PAD // tn
    assert OFF_DT == (nj - 1) * tn
    row = functools.partial(geom.mod_row, rows_per_tile=tm)
    return pl.pallas_call(
        functools.partial(_inproj_kernel, nj),
        out_shape=[jax.ShapeDtypeStruct((geom.t, P_PAD), BF16), jax.ShapeDtypeStruct((geom.t, 128), F32)],
        grid=(geom.t // tm, nj),
        in_specs=[pl.BlockSpec((tm, D), lambda i, j: (i, 0)),
                  pl.BlockSpec((None, None, None, 1, D), lambda i, j: (l, 0, row(i), 0, 0)),
                  pl.BlockSpec((None, None, None, 1, D), lambda i, j: (l, 1, row(i), 0, 0)),
                  pl.BlockSpec((None, D, tn), lambda i, j: (l, 0, j))],
        out_specs=[pl.BlockSpec((tm, tn), lambda i, j: (i, j)),
                   pl.BlockSpec((tm, 128), lambda i, j: (i, 0))],
        scratch_shapes=[pltpu.VMEM((tm, D), BF16)],
        compiler_params=_cp(("parallel", "arbitrary")),
        name="in_proj",
    )(x, mod, mod, w_in_p)


def _dwconv_kernel(geom, act, transpose, tile0, x_ref, xp_ref, xn_ref, w_ref, b_ref, o_ref):
    i = pl.program_id(0) + tile0
    pos = jnp.where(i < geom.npt, i % geom.tpp, (i - geom.npt) % geom.tps)
    last = jnp.where(i < geom.npt, geom.tpp - 1, geom.tps - 1)
    x = x_ref[...].astype(F32)
    rows = lax.broadcasted_iota(jnp.int32, x.shape, 0)
    prev_row = jnp.where(pos == 0, 0.0, xp_ref[...].astype(F32)[HALO - 1:HALO, :])
    next_row = jnp.where(pos == last, 0.0, xn_ref[...].astype(F32)[0:1, :])
    x_prev = jnp.where(rows == 0, prev_row, pltpu.roll(x, 1, 0))
    x_next = jnp.where(rows == TB - 1, next_row, pltpu.roll(x, TB - 1, 0))
    w = w_ref[...]
    y = x_prev * w[0:1, :] + x * w[1:2, :] + x_next * w[2:3, :] + b_ref[...]
    if act:
        y = _silu(y)
    if transpose:
        y = y.T
    o_ref[...] = y.astype(o_ref.dtype)


def dwconv3(geom, u, col0, w, b, wcol0, ncols, *, act, out_dtype=F32, transpose=False, cw=512, group=None):
    nbh = geom.t // HALO
    c0, w0, nj = col0 // cw, wcol0 // cw, ncols // cw
    tile0, ntiles = (0, geom.nt) if group is None else (group[0], group[1] * group[2])
    orow = (lambda i: i) if group is None else (lambda i: (i % group[2]) * group[1] + i // group[2])
    if transpose:
        out_shape = jax.ShapeDtypeStruct((ncols, ntiles * TB), out_dtype)
        out_spec = pl.BlockSpec((cw, TB), lambda i, j: (j, orow(i)))
    else:
        out_shape = jax.ShapeDtypeStruct((ntiles * TB, ncols), out_dtype)
        out_spec = pl.BlockSpec((TB, cw), lambda i, j: (orow(i), j))
    return pl.pallas_call(
        functools.partial(_dwconv_kernel, geom, act, transpose, tile0),
        out_shape=out_shape,
        grid=(ntiles, nj),
        in_specs=[pl.BlockSpec((TB, cw), lambda i, j: (tile0 + i, c0 + j)),
                  pl.BlockSpec((HALO, cw),
                               lambda i, j: (jnp.maximum((tile0 + i) * (TB // HALO) - 1, 0), c0 + j)),
                  pl.BlockSpec((HALO, cw),
                               lambda i, j: (jnp.minimum((tile0 + i + 1) * (TB // HALO), nbh - 1), c0 + j)),
                  pl.BlockSpec((3, cw), lambda i, j: (0, w0 + j)),
                  pl.BlockSpec((1, cw), lambda i, j: (0, w0 + j))],
        out_specs=out_spec,
        compiler_params=_cp(("parallel", "parallel")),
        name="dwconv3",
    )(u, u, u, w, b)


def _dt_kernel(u_ref, bias_ref, nega_ref, dt_ref, la_ref):
    dt = jax.nn.softplus(u_ref[...] + bias_ref[...])
    dt_ref[...] = dt
    la_ref[...] = dt * nega_ref[...]


def ssd_dt(geom, u_dt, dt_bias128, neg_a128):
    tm = 512 if geom.t % 512 == 0 else TB
    return pl.pallas_call(
        _dt_kernel,
        out_shape=[jax.ShapeDtypeStruct((geom.t, 128), F32)] * 2,
        grid=(geom.t // tm,),
        in_specs=[pl.BlockSpec((tm, 128), lambda i: (i, 0)),
                  pl.BlockSpec((1, 128), lambda i: (0, 0)),
                  pl.BlockSpec((1, 128), lambda i: (0, 0))],
        out_specs=[pl.BlockSpec((tm, 128), lambda i: (i, 0))] * 2,
        compiler_params=_cp(("parallel",)),
        name="ssd_dt",
    )(u_dt, dt_bias128, neg_a128)


def _qk_kernel(q_ref, k_ref, cos_ref, sin_ref, qo_ref, kto_ref):
    cos, sin = cos_ref[...], sin_ref[...]

    def rope(x):
        halves = [pltpu.roll(x[:, a:a + 128], 64, 1) for a in (0, 128)]
        return x * cos + jnp.concatenate(halves, axis=1) * sin

    qo_ref[...] = rope(q_ref[...].astype(F32)).astype(BF16)
    kto_ref[...] = (rope(k_ref[...].astype(F32)) * (DK_RET ** -0.5)).T.astype(BF16)


def qk_prep(geom, u, cos_tab, sin_tab):
    tab = lambda i, h: (jnp.where(i < geom.npt, geom.tps, (i - geom.npt) % geom.tps), 0)
    return pl.pallas_call(
        _qk_kernel,
        out_shape=[jax.ShapeDtypeStruct((geom.t, H_RET * DK_RET), BF16),
                   jax.ShapeDtypeStruct((H_RET * DK_RET, geom.t), BF16)],
        grid=(geom.nt, H_RET),
        in_specs=[pl.BlockSpec((TB, DK_RET), lambda i, h: (i, OFF_Q // DK_RET + h)),
                  pl.BlockSpec((TB, DK_RET), lambda i, h: (i, OFF_K // DK_RET + h)),
                  pl.BlockSpec((TB, DK_RET), tab),
                  pl.BlockSpec((TB, DK_RET), tab)],
        out_specs=[pl.BlockSpec((TB, DK_RET), lambda i, h: (i, h)),
                   pl.BlockSpec((DK_RET, TB), lambda i, h: (h, i))],
        compiler_params=_cp(("parallel", "parallel")),
        name="qk_prep",
    )(u, u, cos_tab, sin_tab)


def _scan_kernel(r, p, n, nc, reverse, has_dt, has_init, has_alias, *refs):
    it = iter(refs)
    x_ref, c_ref, bt_ref, la_ref, lat_ref = next(it), next(it), next(it), next(it), next(it)
    dt_ref = next(it) if has_dt else None
    s0_ref = next(it) if has_init else None
    if has_alias:
        next(it)
    y_ref, sf_ref, s_ref, yd_ref = next(it), next(it), next(it), next(it)
    c = pl.program_id(2)
    q = TB

    @pl.when(c == 0)
    def _():
        s_ref[...] = s0_ref[...] if has_init else jnp.zeros_like(s_ref)

    ii = lax.broadcasted_iota(jnp.int32, (q, q), 0)
    jj = lax.broadcasted_iota(jnp.int32, (q, q), 1)
    lower = ii >= jj
    tri_a = (jj >= ii) if reverse else lower
    tri_a_bf = tri_a.astype(BF16)
    tri_b_bf = (lower if reverse else (jj >= ii)).astype(BF16)
    end = 0 if reverse else q - 1

    la_col = la_ref[...]
    la_row = lat_ref[...]
    if r == 1:
        la_col = jnp.broadcast_to(la_col, (q, 128))
        la_row = jnp.broadcast_to(la_row, (8, q))
    cs_col = _dot2_r(tri_a_bf, la_col)[:, :r]
    cs_row = _dot2_l(la_row, tri_b_bf)[:r, :]
    tot = cs_col[end:end + 1, :]
    end_decay = jnp.exp(tot - cs_col)
    in_scale = jnp.exp(cs_col)

    if r == 1:
        expand = lambda a: jnp.broadcast_to(a, (q, p))
    else:
        er = lax.broadcasted_iota(jnp.int32, (r, r * p), 0)
        ec = lax.broadcasted_iota(jnp.int32, (r, r * p), 1)
        e_bf = (ec // p == er).astype(BF16)

        expand = lambda a: _dot2_l(a, e_bf)

    x = x_ref[...].astype(F32)
    if has_dt:
        dt = dt_ref[...]
        xd = x * expand(dt)
        xe = x * expand(dt * end_decay)
    else:
        xd = x
        xe = x * expand(end_decay)
    xd_bf, xe_bf = xd.astype(BF16), xe.astype(BF16)
    c_bf = c_ref[...].astype(BF16)
    bt_bf = bt_ref[...].astype(BF16)

    scores = _dot(c_bf, bt_bf)
    for h in range(r):
        dmat = cs_col[:, h:h + 1] - cs_row[h:h + 1, :]
        wgt = scores * jnp.exp(jnp.where(tri_a, dmat, -1e30))
        yd_ref[:, h * p:(h + 1) * p] = _dot(wgt.astype(BF16), xd_bf[:, h * p:(h + 1) * p])

    in_scale_x = expand(in_scale)
    s_old = s_ref[...]
    y_ref[...] = yd_ref[...] + _dot(c_bf, s_old.astype(BF16)) * in_scale_x
    s_new = s_old * in_scale_x[end:end + 1, :] + _dot(bt_bf, xe_bf)
    s_ref[...] = s_new

    @pl.when(c == nc - 1)
    def _():
        sf_ref[...] = s_new


def chunk_scan(x, xcol0, cm, ccol0, bt, la, lat, dt, s0, y_prev, *, tile0, nseq, nc, g, r, p, n, reverse):
    rp = r * p
    xb, cb = xcol0 // rp, ccol0 // n
    ch = (lambda c: nc - 1 - c) if reverse else (lambda c: c)
    tok = lambda b, c: tile0 + b * nc + ch(c)
    in_specs = [pl.BlockSpec((TB, rp), lambda b, gi, c: (tok(b, c), xb + gi)),
                pl.BlockSpec((TB, n), lambda b, gi, c: (tok(b, c), cb + gi)),
                pl.BlockSpec((n, TB), lambda b, gi, c: (gi, tok(b, c))),
                pl.BlockSpec((None, TB, r), lambda b, gi, c: (gi, tok(b, c), 0)),
                pl.BlockSpec((None, r, TB), lambda b, gi, c: (gi, 0, tok(b, c)))]
    args = [x, cm, bt, la, lat]
    if dt is not None:
        in_specs.append(pl.BlockSpec((None, TB, r), lambda b, gi, c: (gi, tok(b, c), 0)))
        args.append(dt)
    if s0 is not None:
        in_specs.append(pl.BlockSpec((None, None, n, rp), lambda b, gi, c: (b, gi, 0, 0)))
        args.append(s0)
    aliases = {}
    if y_prev is not None:
        in_specs.append(pl.BlockSpec(memory_space=pl.ANY))
        args.append(y_prev)
        aliases = {len(args) - 1: 0}
    return pl.pallas_call(
        functools.partial(_scan_kernel, r, p, n, nc, reverse, dt is not None, s0 is not None, y_prev is not None),
        out_shape=[jax.ShapeDtypeStruct((x.shape[0], g * rp), F32),
                   jax.ShapeDtypeStruct((nseq, g, n, rp), F32)],
        grid=(nseq, g, nc),
        in_specs=in_specs,
        out_specs=[pl.BlockSpec((TB, rp), lambda b, gi, c: (tok(b, c), gi)),
                   pl.BlockSpec((None, None, n, rp), lambda b, gi, c: (b, gi, 0, 0))],
        scratch_shapes=[pltpu.VMEM((n, rp), F32), pltpu.VMEM((TB, rp), F32)],
        input_output_aliases=aliases,
        compiler_params=_cp(("parallel", "parallel", "arbitrary")),
        name="chunk_scan",
    )(*args)


def _filt_hidden_kernel(ls, freq_ref, w1a_ref, w1c_ref, w1s_ref, b1_ref, w2t_ref, b2_ref, sf_ref, o_ref):
    pp = lax.broadcasted_iota(jnp.int32, (1, 2 * ls), 1)
    t = jnp.abs(pp - ls).astype(F32)
    t_norm = t / max(ls - 1, 1)
    ang = (2.0 * math.pi) * t * freq_ref[...] / ls
    pre = (w1a_ref[...] * t_norm
           + jnp.dot(w1c_ref[...], jnp.cos(ang), precision=HIGHEST, preferred_element_type=F32)
           - jnp.dot(w1s_ref[...], jnp.sin(ang), precision=HIGHEST, preferred_element_type=F32)
           + b1_ref[...])
    sf = sf_ref[...]
    h1 = jnp.sin(sf[:, 0:1] * pre)
    h2 = jnp.sin(sf[:, 1:2] * (jnp.dot(w2t_ref[...], h1, precision=HIGHEST, preferred_element_type=F32)
                               + b2_ref[...]))
    o_ref[...] = h2


def _filt_out_kernel(ls, hid_ref, wf_ref, wb_ref, delta_ref, bias_ref, o_ref):
    pp = lax.broadcasted_iota(jnp.int32, (1, 2 * ls), 1)
    t_norm = jnp.abs(pp - ls).astype(F32) / max(ls - 1, 1)
    hid = hid_ref[...]
    kf = jnp.dot(wf_ref[...], hid, precision=HIGHEST, preferred_element_type=F32)
    kb = jnp.dot(wb_ref[...], hid, precision=HIGHEST, preferred_element_type=F32)
    window = jnp.exp(-t_norm * jnp.abs(delta_ref[...]))
    k = jnp.where(pp >= ls, kf, kb) * window
    k = jnp.where(pp == 0, 0.0, k)
    o_ref[...] = k + jnp.where(pp == ls, bias_ref[...], 0.0)


def hyena_filter_rows(ls, hy_w1, hy_b1, hy_w2, hy_b2, hy_w3, hy_freq, hy_bias):
    bands = (POS_EMB - 1) // 2
    freqs = jnp.linspace(1e-4, bands - 1, bands, dtype=F32).reshape(bands, 1)
    w1t = jnp.swapaxes(hy_w1, 1, 2)
    hid = pl.pallas_call(
        functools.partial(_filt_hidden_kernel, ls),
        out_shape=jax.ShapeDtypeStruct((DEPTH, FILT_HID, 2 * ls), F32),
        grid=(DEPTH,),
        in_specs=[pl.BlockSpec((bands, 1), lambda l: (0, 0)),
                  pl.BlockSpec((None, FILT_HID, 1), lambda l: (l, 0, 0)),
                  pl.BlockSpec((None, FILT_HID, bands), lambda l: (l, 0, 0)),
                  pl.BlockSpec((None, FILT_HID, bands), lambda l: (l, 0, 0)),
                  pl.BlockSpec((None, FILT_HID, 1), lambda l: (l, 0, 0)),
                  pl.BlockSpec((None, FILT_HID, FILT_HID), lambda l: (l, 0, 0)),
                  pl.BlockSpec((None, FILT_HID, 1), lambda l: (l, 0, 0)),
                  pl.BlockSpec((None, FILT_HID, 2), lambda l: (l, 0, 0))],
        out_specs=pl.BlockSpec((None, FILT_HID, 2 * ls), lambda l: (l, 0, 0)),
        compiler_params=_cp(("parallel",)),
        name="hyena_filter_hidden",
    )(freqs, w1t[:, :, 0:1], w1t[:, :, 1:1 + bands], w1t[:, :, 1 + bands:], hy_b1[..., None],
      jnp.swapaxes(hy_w2, 1, 2), hy_b2[..., None], jnp.swapaxes(hy_freq, 1, 2))
    w3t = jnp.swapaxes(hy_w3, 1, 2).reshape(DEPTH, 2, 2, W_HY, FILT_HID)
    deltas = jnp.linspace(math.log(HY_TARGET) / HY_SLOW, math.log(HY_TARGET) / HY_FAST, W_HY,
                          dtype=F32).reshape(W_HY, 1)
    cb = 256
    return pl.pallas_call(
        functools.partial(_filt_out_kernel, ls),
        out_shape=jax.ShapeDtypeStruct((DEPTH, 2, W_HY, 2 * ls), F32),
        grid=(DEPTH, 2, W_HY // cb),
        in_specs=[pl.BlockSpec((None, FILT_HID, 2 * ls), lambda l, o, j: (l, 0, 0)),
                  pl.BlockSpec((None, None, None, cb, FILT_HID), lambda l, o, j: (l, o, 0, j, 0)),
                  pl.BlockSpec((None, None, None, cb, FILT_HID), lambda l, o, j: (l, o, 1, j, 0)),
                  pl.BlockSpec((cb, 1), lambda l, o, j: (j, 0)),
                  pl.BlockSpec((None, None, cb, 1), lambda l, o, j: (l, o, j, 0))],
        out_specs=pl.BlockSpec((None, None, cb, 2 * ls), lambda l, o, j: (l, o, j, 0)),
        compiler_params=_cp(("parallel", "parallel", "parallel")),
        name="hyena_filter_rows",
    )(hid, w3t, w3t, deltas, hy_bias[..., None])


def _hyena_kernel(nb, b, cbk, v_ref, x1_ref, x2_ref, w_ref, o_ref, acc_ref):
    ls2 = 2 * nb * TB

    def conv(u, o, ch):
        wrow = w_ref[o, pl.ds(ch, 1), :]
        toep = pltpu.roll(jnp.broadcast_to(wrow, (TB, ls2)), 0, 1, stride=1, stride_axis=0).astype(BF16)
        acc_ref[...] = jnp.zeros_like(acc_ref)
        for mm in range(2 * nb - 1):
            m = mm - (nb - 1)
            lo, hi = max(0, -m), min(nb, nb - m)
            part = _dot(u[lo * b:hi * b, :].astype(BF16), toep[:, TB * (mm + 1):TB * (mm + 2)])
            acc_ref[(lo + m) * b:(hi + m) * b, :] += part
        return acc_ref[...]

    def body(ch, carry):
        z = x1_ref[ch] * conv(v_ref[ch], 0, ch)
        o_ref[ch] = x2_ref[ch] * conv(z, 1, ch)
        return carry

    lax.fori_loop(0, cbk, body, 0, unroll=2)


def hyena_conv(uct, filt_l, nb, b):
    cbk = 8
    rows = nb * b
    return pl.pallas_call(
        functools.partial(_hyena_kernel, nb, b, cbk),
        out_shape=jax.ShapeDtypeStruct((W_HY, rows, TB), F32),
        grid=(W_HY // cbk,),
        in_specs=[pl.BlockSpec((None, cbk, rows, TB), lambda j: (0, j, 0, 0)),
                  pl.BlockSpec((None, cbk, rows, TB), lambda j: (1, j, 0, 0)),
                  pl.BlockSpec((None, cbk, rows, TB), lambda j: (2, j, 0, 0)),
                  pl.BlockSpec((2, cbk, 2 * nb * TB), lambda j: (0, j, 0))],
        out_specs=pl.BlockSpec((cbk, rows, TB), lambda j: (j, 0, 0)),
        scratch_shapes=[pltpu.VMEM((rows, TB), F32)],
        compiler_params=_cp(("parallel",)),
        name="hyena_conv",
    )(uct, uct, uct, filt_l)


def _ssd_post_kernel(yf_ref, yb_ref, xs_ref, z_ref, d_ref, nw_ref, o_ref):
    y = yf_ref[...] + yb_ref[...] + xs_ref[...] * d_ref[...]
    y = y * _silu(z_ref[...].astype(F32))
    y = y * lax.rsqrt(jnp.mean(y * y, axis=-1, keepdims=True) + LN_EPS)
    o_ref[...] = (y * nw_ref[...]).astype(o_ref.dtype)


def ssd_post(geom, yf, yb, xs, u, d_exp, norm_w):
    gw = D_INNER // G_SSD
    tm = 512 if geom.t % 512 == 0 else TB
    blk = pl.BlockSpec((tm, gw), lambda i, gi: (i, gi))
    vec = pl.BlockSpec((1, gw), lambda i, gi: (0, gi))
    return pl.pallas_call(
        _ssd_post_kernel,
        out_shape=jax.ShapeDtypeStruct((geom.t, D_INNER), BF16),
        grid=(geom.t // tm, G_SSD),
        in_specs=[blk, blk, blk, pl.BlockSpec((tm, gw), lambda i, gi: (i, OFF_Z // gw + gi)), vec, vec],
        out_specs=blk,
        compiler_params=_cp(("parallel", "parallel")),
        name="ssd_post",
    )(yf, yb, xs, u, d_exp, norm_w)


def _ret_post_kernel(of_ref, ob_ref, g_ref, w_ref, o_ref):
    o = _ln(of_ref[...] + ob_ref[...]) * w_ref[...]
    o_ref[...] = (o * _silu(g_ref[...].astype(F32))).astype(o_ref.dtype)


def ret_post(geom, of, ob, u, gn_w):
    tm = 512 if geom.t % 512 == 0 else TB
    blk = pl.BlockSpec((tm, DV_RET), lambda i, h: (i, h))
    return pl.pallas_call(
        _ret_post_kernel,
        out_shape=jax.ShapeDtypeStruct((geom.t, H_RET * DV_RET), BF16),
        grid=(geom.t // tm, H_RET),
        in_specs=[blk, blk, pl.BlockSpec((tm, DV_RET), lambda i, h: (i, OFF_G // DV_RET + h)),
                  pl.BlockSpec((1, DV_RET), lambda i, h: (0, h))],
        out_specs=blk,
        compiler_params=_cp(("parallel", "parallel")),
        name="ret_post",
    )(of, ob, u, gn_w)


def _merge_kernel(nsub, *refs):
    yhy_refs = refs[:nsub]
    (yssd_ref, yret_ref, mg0_ref, mg1_ref, mg2_ref, x_ref, g1_ref, sh2_ref, sc2_ref,
     lg_ref, lb_ref, whp_ref, wsp_ref, wrp_ref, wo_ref, rwt_ref, rb_ref, x1_ref, h_ref, aff_ref) = refs[nsub:]
    gate = lambda r: jax.nn.sigmoid(r[...].astype(F32))
    yhy = jnp.concatenate([r[...].astype(BF16) for r in yhy_refs], axis=0)
    merged = (gate(mg0_ref) * _dot(yhy, whp_ref[...])
              + gate(mg1_ref) * _dot(yssd_ref[...], wsp_ref[...])
              + gate(mg2_ref) * _dot(yret_ref[...], wrp_ref[...]))
    mix = _dot(merged.astype(BF16), wo_ref[...])
    x1 = _ln(ALPHA * x_ref[...] + g1_ref[...] * mix) * lg_ref[...] + lb_ref[...]
    x1_ref[...] = x1
    h = _ln(x1) * (1.0 + sc2_ref[...]) + sh2_ref[...]
    h_ref[...] = h.astype(BF16)
    logits = lax.dot_general(rwt_ref[...], h, (((1,), (1,)), ((), ())), precision=HIGHEST,
                             preferred_element_type=F32) + rb_ref[...]
    e = jnp.exp(logits - jnp.max(logits, axis=0, keepdims=True))
    aff_ref[...] = e / jnp.sum(e, axis=0, keepdims=True)


def merge(geom, l, y_hy, y_ssd, y_ret, u, x, mod, ln_g, ln_b, whp, wsp, wrp, wo, rwt, rb):
    tm = 512 if (geom.tp % 512 == 0 and geom.ls % 512 == 0) else TB
    nsub = tm // TB
    row = functools.partial(geom.mod_row, rows_per_tile=tm)

    def hy_tile(t):
        p_blk = (t % geom.tpp) * geom.bp + t // geom.tpp
        ts = t - geom.npt
        s_blk = geom.npt + (ts % geom.tps) * geom.bs + ts // geom.tps
        return jnp.where(t < geom.npt, p_blk, s_blk)

    hys = [pl.BlockSpec((TB, D), lambda i, k=k: (hy_tile(i * nsub + k), 0)) for k in range(nsub)]
    tok = lambda w: pl.BlockSpec((tm, w), lambda i: (i, 0))
    mgs = [pl.BlockSpec((tm, D), lambda i, k=k: (i, OFF_MG // D + k)) for k in range(3)]
    modspec = lambda k: pl.BlockSpec((None, None, None, 1, D), lambda i: (l, k, row(i), 0, 0))
    vec = pl.BlockSpec((None, 1, D), lambda i: (l, 0, 0))
    wspec = lambda kk: pl.BlockSpec((None, kk, D), lambda i: (l, 0, 0), pipeline_mode=pl.Buffered(1))
    return pl.pallas_call(
        functools.partial(_merge_kernel, nsub),
        out_shape=[jax.ShapeDtypeStruct((geom.t, D), F32), jax.ShapeDtypeStruct((geom.t, D), BF16),
                   jax.ShapeDtypeStruct((N_EXPERTS, geom.t), F32)],
        grid=(geom.t // tm,),
        in_specs=hys + [tok(D_INNER), tok(D)] + mgs + [tok(D), modspec(2), modspec(3), modspec(4), vec, vec,
                  wspec(D), wspec(D_INNER), wspec(D), wspec(D),
                  pl.BlockSpec((None, N_EXPERTS, D), lambda i: (l, 0, 0)),
                  pl.BlockSpec((None, N_EXPERTS, 1), lambda i: (l, 0, 0))],
        out_specs=[tok(D), tok(D), pl.BlockSpec((N_EXPERTS, tm), lambda i: (0, i))],
        compiler_params=_cp(("parallel",)),
        name="merge",
    )(*([y_hy] * nsub), y_ssd, y_ret, u, u, u, x, mod, mod, mod, ln_g, ln_b, whp, wsp, wrp, wo, rwt, rb)


def _select_kernel(n, cap, aff_ref, slot_ref, bnd_ref):
    aff = aff_ref[...]
    bits = pltpu.bitcast(aff, jnp.int32)

    def bit_step(k, thr):
        cand = thr | jnp.left_shift(jnp.int32(1), 30 - k)
        cnt = jnp.sum(jnp.where(bits >= cand, 1.0, 0.0), axis=1, keepdims=True)
        return jnp.where(cnt >= cap, cand, thr)

    thr = lax.fori_loop(0, 31, bit_step, jnp.zeros((N_EXPERTS, 1), jnp.int32))
    gt = bits > thr
    eq = bits == thr
    need = cap - jnp.sum(jnp.where(gt, 1.0, 0.0), axis=1, keepdims=True)

    kk = lax.broadcasted_iota(jnp.int32, (TB, TB), 0)
    jj = lax.broadcasted_iota(jnp.int32, (TB, TB), 1)
    before = (kk < jj).astype(BF16)

    def excl_cumsum(mask):
        ones = jnp.where(mask, 1.0, 0.0)
        outs, carry = [], jnp.zeros((N_EXPERTS, 1), F32)
        for j in range(n // TB):
            blk = ones[:, j * TB:(j + 1) * TB]
            outs.append(_dot(blk.astype(BF16), before) + carry)
            carry = carry + jnp.sum(blk, axis=1, keepdims=True)
        return jnp.concatenate(outs, axis=1)

    sel = gt | (eq & (excl_cumsum(eq) < need))
    slot_ref[...] = jnp.where(sel, excl_cumsum(sel), -1.0)
    tt = lax.broadcasted_iota(jnp.int32, (n, 128), 0)
    tj = lax.broadcasted_iota(jnp.int32, (n, 128), 1)
    bnd_ref[...] = _dot(jnp.where(sel, 1.0, 0.0).astype(BF16), (tt < tj * TB).astype(BF16))


def ec_select(aff_t, tile0, nseq, n):
    cap = EC_CAPACITY * n // N_EXPERTS
    nt = n // TB
    assert tile0 % nt == 0 and nt < 128
    return pl.pallas_call(
        functools.partial(_select_kernel, n, cap),
        out_shape=[jax.ShapeDtypeStruct((N_EXPERTS, nseq * n), F32),
                   jax.ShapeDtypeStruct((nseq, N_EXPERTS, 128), F32)],
        grid=(nseq,),
        in_specs=[pl.BlockSpec((N_EXPERTS, n), lambda s: (0, tile0 // nt + s))],
        out_specs=[pl.BlockSpec((N_EXPERTS, n), lambda s: (0, s)),
                   pl.BlockSpec((None, N_EXPERTS, 128), lambda s: (s, 0, 0))],
        compiler_params=_cp(("parallel",)),
        name="ec_select",
    )(aff_t)


def _moe_kernel(nd, ntd, cap, sc, has_alias, b_ref, slot_ref, aff_ref, h_ref, w1_ref, w3_ref, w2_ref, *rest):
    o_ref, xe_ref, g_ref, ye_ref = rest[1:] if has_alias else rest
    blk, e = pl.program_id(0), pl.program_id(1)

    @pl.when(e == 0)
    def _():
        o_ref[...] = jnp.zeros_like(o_ref)

    xe_ref[...] = jnp.zeros_like(xe_ref)
    g_ref[...] = jnp.zeros_like(g_ref)

    def for_pairs(fn):
        def tile_body(t, carry):
            d = 0 if nd == 1 else (t if ntd == 1 else t // ntd)
            slot = slot_ref[e, pl.ds(t, 1), :]
            tok0 = pl.multiple_of(t * TB, TB)
            if ntd > 1:
                base = ((blk * nd + d) * N_EXPERTS + e) * BOUNDS_STRIDE + (t - d * ntd)
                lo, hi = b_ref[base], b_ref[base + 1]
            for c in range(cap // sc):
                row0 = d * cap + c * sc
                if nd > 1:
                    row0 = pl.multiple_of(row0, sc)

                def run(c=c, row0=row0):
                    jcol = (lax.broadcasted_iota(jnp.int32, (sc, 1), 0) + c * sc).astype(F32)
                    fn(t, tok0, row0, slot == jcol)

                if ntd > 1:
                    pl.when((hi > c * sc) & (lo < (c + 1) * sc))(run)
                else:
                    run()
            return carry

        lax.fori_loop(0, nd * ntd, tile_body, 0)

    def gather(t, tok0, row0, pick):
        pick_bf = jnp.where(pick, 1.0, 0.0).astype(BF16)
        gate = aff_ref[e, pl.ds(t, 1), :]
        xe_ref[pl.ds(row0, sc), :] += _dot(pick_bf, h_ref[pl.ds(tok0, TB), :])
        g_ref[pl.ds(row0, sc), :] += jnp.sum(jnp.where(pick, gate, 0.0), axis=1, keepdims=True)

    for_pairs(gather)

    xe = xe_ref[...].astype(BF16)
    hid = _silu(_dot(xe, w1_ref[...])) * _dot(xe, w3_ref[...]) * g_ref[...]
    ye_ref[...] = _dot(hid.astype(BF16), w2_ref[...]).astype(BF16)

    def scatter(t, tok0, row0, pick):
        pick_bf = jnp.where(pick, 1.0, 0.0).astype(BF16)
        o_ref[pl.ds(tok0, TB), :] += lax.dot_general(pick_bf, ye_ref[pl.ds(row0, sc), :], (((0,), (0,)), ((), ())),
                                                     preferred_element_type=F32)

    for_pairs(scatter)


MOE_BLOCK_TOKENS = 4096
BOUNDS_STRIDE = 32


def moe_experts(l, slot, bounds, aff_t, h, w1, w3, w2, ffn_prev, tile0, nseq, n):
    cap = EC_CAPACITY * n // N_EXPERTS
    ntd = n // TB
    nd = max(d for d in range(1, nseq + 1) if nseq % d == 0 and d * n <= MOE_BLOCK_TOKENS)
    tpb = nd * ntd
    assert tile0 % tpb == 0 and tpb % 8 == 0
    sc = min(cap, 128)
    nslot = nd * cap
    slot3 = slot.reshape(N_EXPERTS, nseq * ntd, TB)
    aff3 = aff_t.reshape(N_EXPERTS, aff_t.shape[1] // TB, TB)
    assert ntd + 1 <= BOUNDS_STRIDE
    bflat = (bounds[:, :, :BOUNDS_STRIDE].astype(jnp.int32).reshape(-1) if ntd > 1
             else jnp.zeros((8,), jnp.int32))
    wspec = pl.BlockSpec((None, None, D, D), lambda s, e, b: (l, e, 0, 0))
    in_specs = [pl.BlockSpec((N_EXPERTS, tpb, TB), lambda s, e, b: (0, s, 0)),
                pl.BlockSpec((N_EXPERTS, tpb, TB), lambda s, e, b: (0, tile0 // tpb + s, 0)),
                pl.BlockSpec((tpb * TB, D), lambda s, e, b: (tile0 // tpb + s, 0), pipeline_mode=pl.Buffered(1)),
                wspec, wspec, wspec]
    args = [bflat, slot3, aff3, h, w1, w3, w2]
    aliases = {}
    if ffn_prev is not None:
        in_specs.append(pl.BlockSpec(memory_space=pl.ANY))
        args.append(ffn_prev)
        aliases = {len(args) - 1: 0}
    grid_spec = pltpu.PrefetchScalarGridSpec(
        num_scalar_prefetch=1,
        grid=(nseq // nd, N_EXPERTS),
        in_specs=in_specs,
        out_specs=pl.BlockSpec((tpb * TB, D), lambda s, e, b: (tile0 // tpb + s, 0), pipeline_mode=pl.Buffered(1)),
        scratch_shapes=[pltpu.VMEM((nslot, D), F32), pltpu.VMEM((nslot, 1), F32), pltpu.VMEM((nslot, D), BF16)])
    return pl.pallas_call(
        functools.partial(_moe_kernel, nd, ntd, cap, sc, ffn_prev is not None),
        out_shape=jax.ShapeDtypeStruct((h.shape[0], D), F32),
        grid_spec=grid_spec,
        input_output_aliases=aliases,
        compiler_params=_cp(("parallel", "arbitrary")),
        name="moe_experts",
    )(*args)


def _resid_kernel(x_ref, f_ref, g_ref, lg_ref, lb_ref, o_ref):
    o_ref[...] = _ln(ALPHA * x_ref[...] + g_ref[...] * f_ref[...]) * lg_ref[...] + lb_ref[...]


def resid_ln(geom, l, x1, ffn, mod, ln_g, ln_b):
    tm = 512 if (geom.tp % 512 == 0 and geom.ls % 512 == 0) else TB
    row = functools.partial(geom.mod_row, rows_per_tile=tm)
    tok = pl.BlockSpec((tm, D), lambda i: (i, 0))
    vec = pl.BlockSpec((None, 1, D), lambda i: (l, 0, 0))
    return pl.pallas_call(
        _resid_kernel,
        out_shape=jax.ShapeDtypeStruct((geom.t, D), F32),
        grid=(geom.t // tm,),
        in_specs=[tok, tok, pl.BlockSpec((None, None, None, 1, D), lambda i: (l, 5, row(i), 0, 0)), vec, vec],
        out_specs=tok,
        compiler_params=_cp(("parallel",)),
        name="resid_ln",
    )(x1, ffn, mod, ln_g, ln_b)


def _rope_tables(ls):
    quarter = DK_RET // 4
    inv_freq = ROPE_BASE ** (-jnp.arange(quarter, dtype=F32) / quarter)
    pos = jnp.arange(ls, dtype=jnp.int32)
    row_ang = (pos // GRID_W).astype(F32)[:, None] * inv_freq[None]
    col_ang = (pos % GRID_W).astype(F32)[:, None] * inv_freq[None]
    cos = jnp.concatenate([jnp.cos(row_ang)] * 2 + [jnp.cos(col_ang)] * 2, axis=1)
    sin = jnp.concatenate([-jnp.sin(row_ang), jnp.sin(row_ang), -jnp.sin(col_ang), jnp.sin(col_ang)], axis=1)
    cos = jnp.concatenate([cos, jnp.ones((TB, DK_RET), F32)], axis=0)
    sin = jnp.concatenate([sin, jnp.zeros((TB, DK_RET), F32)], axis=0)
    return cos, sin


def _to_channel_major(uc):
    rows = uc.shape[0]
    return jnp.transpose(uc).reshape(3, W_HY, rows // TB, TB)


def _to_token_major(yt):
    return jnp.transpose(yt.reshape(W_HY, -1))


def _state_to_t(s, g, r):
    b, h, p, n = s.shape
    return jnp.transpose(s.reshape(b, g, r, p, n), (0, 1, 4, 2, 3)).reshape(b, g, n, r * p)


def _state_from_t(st, r):
    b, g, n, rp = st.shape
    p = rp // r
    return jnp.transpose(st.reshape(b, g, n, r, p), (0, 1, 3, 4, 2)).reshape(b, g * r, p, n)


def _per_group(a, g, r):
    t = a.shape[0]
    a3 = a.reshape(t, g, r)
    return jnp.transpose(a3, (1, 0, 2)), jnp.transpose(a3, (1, 2, 0))


def kernel(x_prompt, x_sample, c, state_ssd, state_ret, c_ctx, w_in, ada_w, ada_b, hy_conv_w, hy_conv_b, hy_w1, hy_b1, hy_w2, hy_b2, hy_w3, hy_freq, hy_bias, hy_proj, ssd_conv_w, ssd_conv_b, ssd_dt_bias, ssd_a_log, ssd_d, ssd_norm_w, ssd_proj, ret_decay, ret_gn_w, ret_proj, w_out, ln1_g, ln1_b, ln2_g, ln2_b, router_w, router_b, exp_w1, exp_w3, exp_w2):
    bp, lp, _ = x_prompt.shape
    bs, ls, _ = x_sample.shape
    geom = Geom(bp, lp, bs, ls)
    assert lp % TB == 0 and ls % TB == 0 and bs + 1 <= 16 and bp % 8 == 0 and bs % 8 == 0 and (bp * lp) % ls == 0

    w_in_p = jnp.concatenate([w_in[:, :, :OFF_Q], w_in[:, :, OFF_Q + 64:], w_in[:, :, OFF_Q:OFF_Q + 64],
                              jnp.zeros((DEPTH, D, P_PAD - OFF_DT - 64), F32)], axis=2).astype(BF16)
    whp, wsp, wrp, wo = (w.astype(BF16) for w in (hy_proj, ssd_proj, ret_proj, w_out))
    ew1, ew3, ew2 = (w.astype(BF16) for w in (exp_w1, exp_w3, exp_w2))
    rwt = jnp.swapaxes(router_w, 1, 2)
    rb = router_b[..., None]
    vec3 = lambda a: a[:, None, :]
    cond16 = jnp.zeros((16, D), F32).at[0].set(c_ctx).at[1:1 + bs].set(c)
    mod = ada_mod(cond16, ada_w, ada_b)
    mod = jnp.transpose(mod.reshape(DEPTH, 16, 6, D), (0, 2, 1, 3))[:, :, :, None, :]

    filt_p = hyena_filter_rows(lp, hy_w1, hy_b1, hy_w2, hy_b2, hy_w3, hy_freq, hy_bias)
    filt_s = hyena_filter_rows(ls, hy_w1, hy_b1, hy_w2, hy_b2, hy_w3, hy_freq, hy_bias)
    cos_tab, sin_tab = _rope_tables(ls)
    pad64 = lambda a: jnp.pad(a.reshape(1, 2 * H_SSD), ((0, 0), (0, 128 - 2 * H_SSD)))
    d_exp = jnp.repeat(ssd_d, P_SSD, axis=1)
    ret_la = -jnp.exp(ret_decay)

    x = jnp.concatenate([x_prompt.reshape(geom.tp, D), x_sample.reshape(geom.ts, D)], axis=0)
    ssd_states, ret_states = [], []
    groups = ((0, bp, geom.tpp), (geom.npt, bs, geom.tps))

    for l in range(DEPTH):
        u, u_dt = in_proj(geom, x, mod, l, w_in_p)

        y_parts = []
        for grp, filt in zip(groups, (filt_p, filt_s)):
            uc_hy = dwconv3(geom, u, OFF_HY, hy_conv_w[l], hy_conv_b[l][None], 0, 3 * W_HY, act=False, cw=1024,
                            group=grp)
            yt = hyena_conv(_to_channel_major(uc_hy), filt[l], grp[2], grp[1])
            y_parts.append(_to_token_major(yt))
        y_hy = jnp.concatenate(y_parts, axis=0)

        xs = dwconv3(geom, u, OFF_XBC, ssd_conv_w[l], ssd_conv_b[l][None], 0, D_INNER, act=True, cw=1024)
        bm_t = dwconv3(geom, u, OFF_XBC + D_INNER, ssd_conv_w[l], ssd_conv_b[l][None], D_INNER,
                       G_SSD * N_SSD, act=True, out_dtype=BF16, transpose=True)
        cm = dwconv3(geom, u, OFF_XBC + D_INNER + G_SSD * N_SSD, ssd_conv_w[l], ssd_conv_b[l][None],
                     D_INNER + G_SSD * N_SSD, G_SSD * N_SSD, act=True, out_dtype=BF16)
        dt, la = ssd_dt(geom, u_dt, pad64(ssd_dt_bias[l]), pad64(-jnp.exp(ssd_a_log[l])))
        y_ssd_dir, s_ssd_dir = [], []
        for d in range(2):
            la_g, la_gt = _per_group(la[:, d * H_SSD:(d + 1) * H_SSD], G_SSD, R_SSD)
            dt_g, _ = _per_group(dt[:, d * H_SSD:(d + 1) * H_SSD], G_SSD, R_SSD)
            y = None
            for gi, (tile0, nseq, nc) in enumerate(groups):
                s0 = None if gi == 0 else _state_to_t(state_ssd[:, l, d], G_SSD, R_SSD)
                y, sf = chunk_scan(xs, 0, cm, 0, bm_t, la_g, la_gt, dt_g, s0, y, tile0=tile0, nseq=nseq, nc=nc,
                                   g=G_SSD, r=R_SSD, p=P_SSD, n=N_SSD, reverse=(d == 1))
                if gi == 0:
                    s_ssd_dir.append(_state_from_t(sf, R_SSD))
            y_ssd_dir.append(y)
        ssd_states.append(jnp.stack(s_ssd_dir, axis=1))
        y_ssd = ssd_post(geom, y_ssd_dir[0], y_ssd_dir[1], xs, u, d_exp[l][None], ssd_norm_w[l][None])

        q_r, k_t = qk_prep(geom, u, cos_tab, sin_tab)
        y_ret_dir, s_ret_dir = [], []
        for d in range(2):
            la_tok = jnp.broadcast_to(ret_la[l, d][None, :], (geom.t, H_RET))
            la_g, la_gt = _per_group(la_tok, H_RET, 1)
            y = None
            for gi, (tile0, nseq, nc) in enumerate(groups):
                s0 = None if gi == 0 else _state_to_t(state_ret[:, l, d], H_RET, 1)
                y, sf = chunk_scan(u, OFF_V, q_r, 0, k_t, la_g, la_gt, None, s0, y, tile0=tile0, nseq=nseq, nc=nc,
                                   g=H_RET, r=1, p=DV_RET, n=DK_RET, reverse=(d == 1))
                if gi == 0:
                    s_ret_dir.append(_state_from_t(sf, 1))
            y_ret_dir.append(y)
        ret_states.append(jnp.stack(s_ret_dir, axis=1))
        y_ret = ret_post(geom, y_ret_dir[0], y_ret_dir[1], u, ret_gn_w[l][None])

        x1, h_ffn, aff_t = merge(geom, l, y_hy, y_ssd, y_ret, u, x, mod, vec3(ln1_g), vec3(ln1_b),
                                 whp, wsp, wrp, wo, rwt, rb)

        ffn = None
        for (tile0, nseq, nc) in groups:
            n = nc * TB
            slot, bounds = ec_select(aff_t, tile0, nseq, n)
            ffn = moe_experts(l, slot, bounds, aff_t, h_ffn, ew1, ew3, ew2, ffn, tile0, nseq, n)
        x = resid_ln(geom, l, x1, ffn, mod, vec3(ln2_g), vec3(ln2_b))

    y_prompt = x[:geom.tp].reshape(bp, lp, D)
    y_sample = x[geom.tp:].reshape(bs, ls, D)
    return (y_prompt, y_sample, jnp.stack(ssd_states, axis=1), jnp.stack(ret_states, axis=1))
```
